```python
import math
import jax
import jax.numpy as jnp
from jax import lax
import numpy as np

D_MODEL = 2048
BATCH = 16
SEQ = 2048
DEPTH = 2
DEC_BATCH = 4
DEC_SEQ = 4096
PAST_LEN = 128

GRID_W = 64
NA_HEADS = 4
NA_HEAD_DIM = 128
NA_KH_MAX = 8
NA_KW = 16
NA_QW = NA_KW
NA_BAND = 2 * NA_KW
RNN_WIDTH = D_MODEL // 2
RNN_BLOCKS = 8
RNN_BLOCK_W = RNN_WIDTH // RNN_BLOCKS
CONV_W = 4
LRU_C = 8.0
DIFF_HEADS = 4
DIFF_HEAD_DIM = 64
Q_BLOCK = 128
N_EXPERTS = 32
TOP_K = 4
D_FF = D_MODEL
SWIGLU_LIMIT = 7.0
SWIGLU_ALPHA = 1.702
MOE_BLOCK = 128
DN_ALPHA = (2 * DEPTH) ** 0.25
DN_BETA = (8 * DEPTH) ** -0.25
LN_EPS = 1e-5
NEG_INF = -1e30

NA_WIDTH = NA_HEADS * NA_HEAD_DIM
DIFF_QK = DIFF_HEADS * 2 * DIFF_HEAD_DIM
DIFF_V = DIFF_HEADS * 2 * DIFF_HEAD_DIM
IN_SPLITS = (NA_WIDTH, NA_WIDTH, NA_WIDTH, RNN_WIDTH, RNN_WIDTH, DIFF_QK, DIFF_QK, DIFF_V, D_MODEL, D_MODEL, D_MODEL)
IN_WIDTH = sum(IN_SPLITS)
IN_OFFSETS = [sum(IN_SPLITS[:i + 1]) for i in range(len(IN_SPLITS) - 1)]

kernel_name = 'hybrid_natten_rglru_diffattn_moe_encoder'


def layer_norm(x, g, b):
    xf = x.astype(jnp.float32)
    mu = jnp.mean(xf, -1, keepdims=True)
    var = jnp.mean(jnp.square(xf - mu), -1, keepdims=True)
    return ((xf - mu) * lax.rsqrt(var + LN_EPS)).astype(x.dtype) * g + b


def rms_norm(x, g):
    xf = x.astype(jnp.float32)
    return (xf * lax.rsqrt(jnp.mean(xf * xf, -1, keepdims=True) + LN_EPS)).astype(x.dtype) * g


def neighbourhood_attention(q, k, v, rpb):
    B, S, H, Dh = q.shape
    rows = S // GRID_W
    kh = min(NA_KH_MAX, rows)
    n_cb = GRID_W // NA_QW
    cb = np.arange(n_cb)
    band_start = np.clip(cb * NA_QW - NA_KW // 2, 0, GRID_W - NA_BAND)
    band_cols = band_start[:, None] + np.arange(NA_BAND)
    q_cols = cb[:, None] * NA_QW + np.arange(NA_QW)
    win_start = np.clip(q_cols - NA_KW // 2, 0, GRID_W - NA_KW)
    kc = band_cols[:, None, :]
    col_ok = (kc >= win_start[..., None]) & (kc < win_start[..., None] + NA_KW)
    col_mask = jnp.where(jnp.asarray(col_ok), 0.0, NEG_INF).astype(jnp.float32)[:, :, None, :]
    col_off = np.clip(kc - q_cols[..., None] + NA_KW - 1, 0, 2 * NA_KW - 2)
    rpb_col = rpb[:, :, col_off].astype(jnp.float32)
    q5 = q.reshape(B, rows, n_cb, NA_QW, H, Dh)
    k_band = jnp.take(k.reshape(B, rows, GRID_W, H, Dh), band_cols, axis=2)
    v_band = jnp.take(v.reshape(B, rows, GRID_W, H, Dh), band_cols, axis=2)
    scale = Dh ** -0.5

    def row_block(i):
        r0 = jnp.clip(i - kh // 2, 0, rows - kh)
        kr = lax.dynamic_slice_in_dim(k_band, r0, kh, axis=1)
        vr = lax.dynamic_slice_in_dim(v_band, r0, kh, axis=1)
        qi = lax.dynamic_index_in_dim(q5, i, axis=1, keepdims=False)
        s = jnp.einsum('bcqhd,brcjhd->bhcqrj', qi, kr).astype(jnp.float32) * scale
        row_off = r0 + jnp.arange(kh) - i + NA_KH_MAX - 1
        bias = jnp.take(rpb_col, row_off, axis=1).transpose(0, 2, 3, 1, 4)
        s = s + bias + col_mask
        p = jax.nn.softmax(s.reshape(B, H, n_cb, NA_QW, kh * NA_BAND), axis=-1)
        p = p.reshape(B, H, n_cb, NA_QW, kh, NA_BAND).astype(v.dtype)
        o = jnp.einsum('bhcqrj,brcjhd->bcqhd', p, vr)
        return o.reshape(B, GRID_W, H, Dh)

    out = lax.map(row_block, jnp.arange(rows))
    return out.transpose(1, 0, 2, 3, 4).reshape(B, S, H * Dh)


def _lin_combine(e1, e2):
    a1, b1 = e1
    a2, b2 = e2
    return (a1 * a2, a2 * b1 + b2)


def rglru_branch(xr, yr, conv_w, conv_b, wa, ba, wx, bx, lam):
    B, S, C = xr.shape
    xc = lax.conv_general_dilated(
        xr, conv_w[:, None, :], window_strides=(1,),
        padding=[(CONV_W // 2, CONV_W - 1 - CONV_W // 2)],
        dimension_numbers=('NWC', 'WIO', 'NWC'), feature_group_count=C) + conv_b
    xb = xc.reshape(B, S, RNN_BLOCKS, RNN_BLOCK_W)
    xcf = xc.astype(jnp.float32)
    hs = []
    for d in range(2):
        r = jax.nn.sigmoid((jnp.einsum('bsnk,nkj->bsnj', xb, wa[d]).reshape(B, S, C) + ba[d]).astype(jnp.float32))
        ig = jax.nn.sigmoid((jnp.einsum('bsnk,nkj->bsnj', xb, wx[d]).reshape(B, S, C) + bx[d]).astype(jnp.float32))
        log_a = -LRU_C * jax.nn.softplus(-lam[d].astype(jnp.float32)) * r
        a = jnp.exp(log_a)
        b = jnp.sqrt(-jnp.expm1(2.0 * log_a)) * (ig * xcf)
        _, h = lax.associative_scan(_lin_combine, (a, b), reverse=(d == 1), axis=1)
        hs.append(h)
    h = (hs[0] + hs[1]).astype(xr.dtype)
    return h * jax.nn.gelu(yr)


def diff_attention(q, k, v, lam, lam_init, subln_g):
    B, S, H, _, dc = q.shape
    n_qb = S // Q_BLOCK
    scale = dc ** -0.5
    slopes = 2.0 ** (-8.0 * (jnp.arange(H, dtype=jnp.float32) + 1.0) / H)
    kpos = jnp.arange(S)
    qb = q.reshape(B, n_qb, Q_BLOCK, H, 2, dc).transpose(1, 0, 2, 3, 4, 5)

    def q_block(args):
        qi, bi = args
        s = jnp.einsum('bqhmd,bkhmd->bmhqk', qi, k).astype(jnp.float32) * scale
        qpos = bi * Q_BLOCK + jnp.arange(Q_BLOCK)
        dist = jnp.abs(qpos[:, None] - kpos[None, :]).astype(jnp.float32)
        s = s - slopes[:, None, None] * dist
        p = jax.nn.softmax(s, axis=-1)
        attn = p[:, 0] - lam * p[:, 1]
        return jnp.einsum('bhqk,bkhe->bqhe', attn.astype(v.dtype), v)

    out = lax.map(q_block, (qb, jnp.arange(n_qb)))
    out = out.transpose(1, 0, 2, 3, 4).reshape(B, S, H, 2 * dc)
    out = rms_norm(out, subln_g) * (1.0 - lam_init)
    return out.reshape(B, S, H * 2 * dc)


def moe(x, w_router, b_router, w1, b1, w2, b2):
    B, S, D = x.shape
    xf = x.reshape(-1, D)
    n = xf.shape[0]
    logits = (xf @ w_router + b_router).astype(jnp.float32)
    top_v, top_i = lax.top_k(logits, TOP_K)
    top_w = jax.nn.softmax(top_v, axis=-1)
    n_asg = n * TOP_K
    e_flat = top_i.reshape(-1)
    w_flat = top_w.reshape(-1)
    tok_flat = jnp.arange(n_asg) // TOP_K
    order = jnp.argsort(e_flat)
    e_sorted = e_flat[order]
    tok_sorted = tok_flat[order]
    w_sorted = w_flat[order]
    counts = jnp.bincount(e_flat, length=N_EXPERTS)
    padded = (counts + MOE_BLOCK - 1) // MOE_BLOCK * MOE_BLOCK
    starts = jnp.cumsum(counts) - counts
    pends = jnp.cumsum(padded)
    pstarts = pends - padded
    dest = pstarts[e_sorted] + jnp.arange(n_asg) - starts[e_sorted]
    n_rows = n_asg + N_EXPERTS * MOE_BLOCK
    n_blocks = n_rows // MOE_BLOCK
    row_tok = jnp.zeros((n_rows,), jnp.int32).at[dest].set(tok_sorted.astype(jnp.int32))
    row_w = jnp.zeros((n_rows,), jnp.float32).at[dest].set(w_sorted)
    block_exp = jnp.clip(jnp.searchsorted(pends, jnp.arange(n_blocks) * MOE_BLOCK, side='right'), 0, N_EXPERTS - 1)
    xrows = xf[row_tok].reshape(n_blocks, MOE_BLOCK, D)

    def expert_block(args):
        xb, e = args
        h = xb @ w1[e] + b1[e]
        glu = jnp.minimum(h[:, 0::2], SWIGLU_LIMIT)
        lin = jnp.clip(h[:, 1::2], -SWIGLU_LIMIT, SWIGLU_LIMIT)
        act = glu * jax.nn.sigmoid(SWIGLU_ALPHA * glu) * (lin + 1.0)
        return act @ w2[e] + b2[e]

    yrows = lax.map(expert_block, (xrows, block_exp)).reshape(n_rows, D)
    y = jnp.zeros_like(xf).at[row_tok].add(yrows * row_w[:, None].astype(yrows.dtype))
    return y.reshape(B, S, D)


def encoder_layer(x, layer, p):
    B, S, D = x.shape
    proj = x @ p['w_in']
    qa, ka, va, xr, yr, qc, kc, vc, ga, gb, gc = jnp.split(proj, IN_OFFSETS, axis=-1)
    oa = neighbourhood_attention(
        qa.reshape(B, S, NA_HEADS, NA_HEAD_DIM), ka.reshape(B, S, NA_HEADS, NA_HEAD_DIM),
        va.reshape(B, S, NA_HEADS, NA_HEAD_DIM), p['rpb_a'])
    ob = rglru_branch(xr, yr, p['conv_w'], p['conv_b'], p['lru_wa'], p['lru_ba'],
                      p['lru_wx'], p['lru_bx'], p['lru_lambda'])
    lam_init = 0.8 - 0.6 * math.exp(-0.3 * layer)
    lam = (jnp.exp(jnp.sum(p['lam_q1'].astype(jnp.float32) * p['lam_k1'].astype(jnp.float32)))
           - jnp.exp(jnp.sum(p['lam_q2'].astype(jnp.float32) * p['lam_k2'].astype(jnp.float32))) + lam_init)
    oc = diff_attention(
        qc.reshape(B, S, DIFF_HEADS, 2, DIFF_HEAD_DIM), kc.reshape(B, S, DIFF_HEADS, 2, DIFF_HEAD_DIM),
        vc.reshape(B, S, DIFF_HEADS, 2 * DIFF_HEAD_DIM), lam, lam_init, p['subln_g'])
    merged = (jax.nn.sigmoid(ga) * (oa @ p['w_proj_a'])
              + jax.nn.sigmoid(gb) * (ob @ p['w_proj_b'])
              + jax.nn.sigmoid(gc) * (oc @ p['w_proj_c']))
    x = layer_norm(DN_ALPHA * x + merged @ p['w_out'], p['ln1_g'], p['ln1_b'])
    y = moe(x, p['w_router'], p['b_router'], p['w1'], p['b1'], p['w2'], p['b2'])
    return layer_norm(DN_ALPHA * x + y, p['ln2_g'], p['ln2_b'])


def trunk(x, params):
    for l in range(DEPTH):
        p = {name: arr[l] for name, arr in params.items()}
        x = encoder_layer(x, l, p)
    return x


def setup_inputs(seed: int = 0) -> dict:
    key = jax.random.key(seed)
    ks = jax.random.split(key, 32)
    f32 = jnp.float32
    nrm = lambda k, shape, s: jax.random.normal(k, shape, f32) * s
    u = jax.random.uniform(ks[10], (DEPTH, 2, RNN_WIDTH), f32, minval=0.9, maxval=0.999)
    a0 = u ** (1.0 / LRU_C)
    return {
        'x_prompt': nrm(ks[0], (BATCH, SEQ, D_MODEL), 1.0),
        'x_sample': nrm(ks[1], (DEC_BATCH, DEC_SEQ, D_MODEL), 1.0),
        'w_in': nrm(ks[2], (DEPTH, D_MODEL, IN_WIDTH), D_MODEL ** -0.5),
        'rpb_a': nrm(ks[3], (DEPTH, NA_HEADS, 2 * NA_KH_MAX - 1, 2 * NA_KW - 1), 0.1),
        'conv_w': nrm(ks[4], (DEPTH, CONV_W, RNN_WIDTH), CONV_W ** -0.5),
        'conv_b': nrm(ks[5], (DEPTH, RNN_WIDTH), 0.01),
        'lru_wa': nrm(ks[6], (DEPTH, 2, RNN_BLOCKS, RNN_BLOCK_W, RNN_BLOCK_W), RNN_BLOCK_W ** -0.5),
        'lru_ba': nrm(ks[7], (DEPTH, 2, RNN_WIDTH), 0.01),
        'lru_wx': nrm(ks[8], (DEPTH, 2, RNN_BLOCKS, RNN_BLOCK_W, RNN_BLOCK_W), RNN_BLOCK_W ** -0.5),
        'lru_bx': nrm(ks[9], (DEPTH, 2, RNN_WIDTH), 0.01),
        'lru_lambda': jnp.log(a0) - jnp.log1p(-a0),
        'lam_q1': nrm(ks[11], (DEPTH, DIFF_HEAD_DIM), 0.1),
        'lam_k1': nrm(ks[12], (DEPTH, DIFF_HEAD_DIM), 0.1),
        'lam_q2': nrm(ks[13], (DEPTH, DIFF_HEAD_DIM), 0.1),
        'lam_k2': nrm(ks[14], (DEPTH, DIFF_HEAD_DIM), 0.1),
        'subln_g': 1.0 + nrm(ks[15], (DEPTH, 2 * DIFF_HEAD_DIM), 0.02),
        'w_proj_a': nrm(ks[16], (DEPTH, NA_WIDTH, D_MODEL), DN_BETA * NA_WIDTH ** -0.5),
        'w_proj_b': nrm(ks[17], (DEPTH, RNN_WIDTH, D_MODEL), DN_BETA * RNN_WIDTH ** -0.5),
        'w_proj_c': nrm(ks[18], (DEPTH, DIFF_V, D_MODEL), DN_BETA * DIFF_V ** -0.5),
        'w_out': nrm(ks[19], (DEPTH, D_MODEL, D_MODEL), DN_BETA * D_MODEL ** -0.5),
        'ln1_g': 1.0 + nrm(ks[20], (DEPTH, D_MODEL), 0.02),
        'ln1_b': nrm(ks[21], (DEPTH, D_MODEL), 0.02),
        'w_router': nrm(ks[22], (DEPTH, D_MODEL, N_EXPERTS), D_MODEL ** -0.5),
        'b_router': nrm(ks[23], (DEPTH, N_EXPERTS), 0.01),
        'w1': nrm(ks[24], (DEPTH, N_EXPERTS, D_MODEL, 2 * D_FF), D_MODEL ** -0.5),
        'b1': nrm(ks[25], (DEPTH, N_EXPERTS, 2 * D_FF), 0.01),
        'w2': nrm(ks[26], (DEPTH, N_EXPERTS, D_FF, D_MODEL), DN_BETA * D_FF ** -0.5),
        'b2': nrm(ks[27], (DEPTH, N_EXPERTS, D_MODEL), 0.01),
        'ln2_g': 1.0 + nrm(ks[28], (DEPTH, D_MODEL), 0.02),
        'ln2_b': nrm(ks[29], (DEPTH, D_MODEL), 0.02),
    }


def reference(x_prompt, x_sample, w_in, rpb_a, conv_w, conv_b, lru_wa, lru_ba, lru_wx, lru_bx,
              lru_lambda, lam_q1, lam_k1, lam_q2, lam_k2, subln_g, w_proj_a, w_proj_b, w_proj_c,
              w_out, ln1_g, ln1_b, w_router, b_router, w1, b1, w2, b2, ln2_g, ln2_b):
    params = dict(w_in=w_in, rpb_a=rpb_a, conv_w=conv_w, conv_b=conv_b, lru_wa=lru_wa, lru_ba=lru_ba,
                  lru_wx=lru_wx, lru_bx=lru_bx, lru_lambda=lru_lambda, lam_q1=lam_q1, lam_k1=lam_k1,
                  lam_q2=lam_q2, lam_k2=lam_k2, subln_g=subln_g, w_proj_a=w_proj_a, w_proj_b=w_proj_b,
                  w_proj_c=w_proj_c, w_out=w_out, ln1_g=ln1_g, ln1_b=ln1_b, w_router=w_router,
                  b_router=b_router, w1=w1, b1=b1, w2=w2, b2=b2, ln2_g=ln2_g, ln2_b=ln2_b)
    y_prompt = trunk(x_prompt, params)
    y_sample = trunk(x_sample, params)
    return (y_prompt, y_sample)
```

```python
import functools
import math

import jax
import jax.numpy as jnp
from jax import lax
from jax.experimental import pallas as pl
from jax.experimental.pallas import tpu as pltpu

F32 = jnp.float32
BF16 = jnp.bfloat16
I32 = jnp.int32

GRID_W = 64
NA_KH_MAX = 8
NA_KW = 16
LRU_C = 8.0
TOP_K = 4
SWIGLU_LIMIT = 7.0
SWIGLU_ALPHA = 1.702
LN_EPS = 1e-5
NEG_INF = -1e30

LANES = 128
SUBLANES = 8
VMEM_LIMIT = 56 * 1024 * 1024


def _cparams(*sem):
    return pltpu.CompilerParams(dimension_semantics=sem, vmem_limit_bytes=VMEM_LIMIT)


def _resident(block_shape, index_map):
    return pl.BlockSpec(block_shape, index_map, pipeline_mode=pl.Buffered(1))


def _in_proj_kernel(x_ref, w_ref, o_ref, xb_ref):
    @pl.when(pl.program_id(1) == 0)
    def _():
        xb_ref[...] = x_ref[...].astype(BF16)

    o_ref[...] = jnp.dot(xb_ref[...], w_ref[...], preferred_element_type=F32).astype(o_ref.dtype)


def _in_proj(x, w, tm, tn):
    n, d = x.shape
    width = w.shape[1]
    return pl.pallas_call(
        _in_proj_kernel,
        grid=(n // tm, width // tn),
        in_specs=[pl.BlockSpec((tm, d), lambda i, j: (i, 0)),
                  pl.BlockSpec((d, tn), lambda i, j: (0, j))],
        out_specs=pl.BlockSpec((tm, tn), lambda i, j: (i, j)),
        out_shape=jax.ShapeDtypeStruct((n, width), BF16),
        scratch_shapes=[pltpu.VMEM((tm, d), BF16)],
        compiler_params=_cparams("parallel", "arbitrary"),
        name="in_proj",
    )(x, w)


def _na_bias_table(rpb, kh):
    d = jnp.arange(kh)[:, None, None, None]
    c = jnp.arange(GRID_W)[None, :, None, None]
    r = jnp.arange(kh)[None, None, :, None]
    kc = jnp.arange(GRID_W)[None, None, None, :]
    row_off = r - d + NA_KH_MAX - 1
    col_off = jnp.clip(kc - c + NA_KW - 1, 0, 2 * NA_KW - 2)
    win = jnp.clip(c - NA_KW // 2, 0, GRID_W - NA_KW)
    ok = (kc >= win) & (kc < win + NA_KW)
    row_off, col_off, ok = jnp.broadcast_arrays(row_off, col_off, ok)
    bias = rpb.astype(F32)[:, row_off, col_off]
    bias = jnp.where(ok[None], bias, NEG_INF)
    return bias.reshape(rpb.shape[0], kh, GRID_W, kh * GRID_W)


def _na_kernel(q_ref, k_ref, v_ref, bias_ref, o_ref, *, rows, kh, scale):
    def row_body(i, carry):
        r0 = jnp.clip(i - kh // 2, 0, rows - kh)
        q = q_ref[pl.ds(pl.multiple_of(i * GRID_W, GRID_W), GRID_W), :]
        koff = pl.multiple_of(r0 * GRID_W, GRID_W)
        kw = k_ref[pl.ds(koff, kh * GRID_W), :]
        vw = v_ref[pl.ds(koff, kh * GRID_W), :]
        s = lax.dot_general(q, kw, (((1,), (1,)), ((), ())), preferred_element_type=F32)
        s = s * scale + bias_ref[i - r0]
        m = jnp.max(s, axis=-1, keepdims=True)
        p = jnp.exp(s - m)
        l = jnp.sum(p, axis=-1, keepdims=True)
        o = jnp.dot(p.astype(BF16), vw, preferred_element_type=F32) / l
        o_ref[pl.ds(pl.multiple_of(i * GRID_W, GRID_W), GRID_W), :] = o.astype(o_ref.dtype)
        return carry

    lax.fori_loop(0, rows, row_body, 0)


def _na_attention(proj, bias, *, row0, batch, seq, heads, col_q, col_k, col_v):
    hd = LANES
    rows = seq // GRID_W
    kh = bias.shape[1]
    rb = row0 // seq
    assert row0 % seq == 0 and rows >= kh
    spec = lambda col: pl.BlockSpec((seq, hd), lambda b, h: (rb + b, col // hd + h))
    return pl.pallas_call(
        functools.partial(_na_kernel, rows=rows, kh=kh, scale=hd ** -0.5),
        grid=(batch, heads),
        in_specs=[spec(col_q), spec(col_k), spec(col_v),
                  pl.BlockSpec((None, kh, GRID_W, kh * GRID_W), lambda b, h: (h, 0, 0, 0))],
        out_specs=pl.BlockSpec((seq, hd), lambda b, h: (b, h)),
        out_shape=jax.ShapeDtypeStruct((batch * seq, heads * hd), BF16),
        compiler_params=_cparams("parallel", "parallel"),
        name="na_attention",
    )(proj, proj, proj, bias)


def _lru_kernel(x_ref, y_ref, cw_ref, cb_ref, wa_ref, ba_ref, wx_ref, bx_ref, ls_ref, o_ref,
                a0_ref, b0_ref, a1_ref, b1_ref, *, seq):
    chunk = seq // SUBLANES
    x = x_ref[...].astype(F32)
    t_idx = lax.broadcasted_iota(I32, x.shape, 0)
    xc = (cw_ref[0:1, :] * jnp.where(t_idx >= 2, pltpu.roll(x, 2, 0), 0.0)
          + cw_ref[1:2, :] * jnp.where(t_idx >= 1, pltpu.roll(x, 1, 0), 0.0)
          + cw_ref[2:3, :] * x
          + cw_ref[3:4, :] * jnp.where(t_idx < seq - 1, pltpu.roll(x, seq - 1, 0), 0.0)
          + cb_ref[...])
    xb = xc.astype(BF16)
    for d, (a_ref, b_ref) in enumerate(((a0_ref, b0_ref), (a1_ref, b1_ref))):
        r = jax.nn.sigmoid(jnp.dot(xb, wa_ref[d], preferred_element_type=F32) + ba_ref[d:d + 1, :])
        ig = jax.nn.sigmoid(jnp.dot(xb, wx_ref[d], preferred_element_type=F32) + bx_ref[d:d + 1, :])
        a = jnp.exp(ls_ref[d:d + 1, :] * r)
        a_ref[...] = a
        b_ref[...] = jnp.sqrt(1.0 - a * a) * (ig * xc)

    def scan_body(t, carry):
        hf, pf, hb, pb = carry
        tb = chunk - 1 - t
        af = a0_ref[pl.ds(t, SUBLANES, stride=chunk), :]
        bf = b0_ref[pl.ds(t, SUBLANES, stride=chunk), :]
        ab = a1_ref[pl.ds(tb, SUBLANES, stride=chunk), :]
        bb = b1_ref[pl.ds(tb, SUBLANES, stride=chunk), :]
        hf = af * hf + bf
        pf = pf * af
        hb = ab * hb + bb
        pb = pb * ab
        b0_ref[pl.ds(t, SUBLANES, stride=chunk), :] = hf
        a0_ref[pl.ds(t, SUBLANES, stride=chunk), :] = pf
        b1_ref[pl.ds(tb, SUBLANES, stride=chunk), :] = hb
        a1_ref[pl.ds(tb, SUBLANES, stride=chunk), :] = pb
        return hf, pf, hb, pb

    zeros = jnp.zeros((SUBLANES, LANES), F32)
    ones = jnp.ones((SUBLANES, LANES), F32)
    hf, pf, hb, pb = lax.fori_loop(0, chunk, scan_body, (zeros, ones, zeros, ones), unroll=8)

    cf = [jnp.zeros((1, LANES), F32)]
    for j in range(SUBLANES - 1):
        cf.append(pf[j:j + 1] * cf[j] + hf[j:j + 1])
    cb = [jnp.zeros((1, LANES), F32)]
    for j in range(SUBLANES - 1, 0, -1):
        cb.append(pb[j:j + 1] * cb[-1] + hb[j:j + 1])
    cb = cb[::-1]
    for j in range(SUBLANES):
        sl = pl.ds(j * chunk, chunk)
        h = (b0_ref[sl, :] + a0_ref[sl, :] * cf[j]) + (b1_ref[sl, :] + a1_ref[sl, :] * cb[j])
        y = y_ref[sl, :].astype(F32)
        o_ref[sl, :] = (h * jax.nn.gelu(y, approximate=True)).astype(o_ref.dtype)


def _rglru(proj, conv_w, conv_b, wa, ba, wx, bx, log_scale, *, row0, batch, seq, col_x, col_y):
    nb = wa.shape[1]
    bw = wa.shape[2]
    assert bw == LANES and row0 % seq == 0 and seq % (SUBLANES * SUBLANES) == 0
    rb = row0 // seq
    cvec = lambda rows_: pl.BlockSpec((rows_, bw), lambda b, n: (0, n))
    wspec = pl.BlockSpec((2, None, bw, bw), lambda b, n: (0, n, 0, 0))
    return pl.pallas_call(
        functools.partial(_lru_kernel, seq=seq),
        grid=(batch, nb),
        in_specs=[pl.BlockSpec((seq, bw), lambda b, n: (rb + b, col_x // bw + n)),
                  pl.BlockSpec((seq, bw), lambda b, n: (rb + b, col_y // bw + n)),
                  cvec(conv_w.shape[0]), cvec(1), wspec, cvec(2), wspec, cvec(2), cvec(2)],
        out_specs=pl.BlockSpec((seq, bw), lambda b, n: (b, n)),
        out_shape=jax.ShapeDtypeStruct((batch * seq, nb * bw), BF16),
        scratch_shapes=[pltpu.VMEM((seq, bw), F32)] * 4,
        compiler_params=_cparams("parallel", "parallel"),
        name="rglru",
    )(proj, proj, conv_w, conv_b, wa, ba, wx, bx, log_scale)


def _diff_kernel(sc_ref, q_ref, k_ref, v_ref, g_ref, o_ref, *, tq, seq, dc, scale):
    h = pl.program_id(1)
    lam = sc_ref[0]
    out_scale = sc_ref[1]
    slope = sc_ref[2 + h]
    q = q_ref[...]
    k = k_ref[...]
    v = v_ref[...]
    lane = lax.broadcasted_iota(I32, q.shape, 1)
    qpos = pl.program_id(2) * tq + lax.broadcasted_iota(I32, (tq, 1), 0)
    kpos = lax.broadcasted_iota(I32, (1, seq), 1)
    bias = jnp.abs(qpos - kpos).astype(F32) * (-slope)
    outs = []
    for m_idx in range(2):
        in_map = (lane < dc) if m_idx == 0 else (lane >= dc)
        qm = jnp.where(in_map, q, jnp.zeros_like(q))
        s = lax.dot_general(qm, k, (((1,), (1,)), ((), ())), preferred_element_type=F32)
        s = s * scale + bias
        mx = jnp.max(s, axis=-1, keepdims=True)
        p = jnp.exp(s - mx)
        l = jnp.sum(p, axis=-1, keepdims=True)
        outs.append(jnp.dot(p.astype(BF16), v, preferred_element_type=F32) / l)
    o = outs[0] - lam * outs[1]
    o = o * lax.rsqrt(jnp.mean(o * o, axis=-1, keepdims=True) + LN_EPS)
    o_ref[...] = (o * g_ref[...] * out_scale).astype(o_ref.dtype)


def _diff_attention(proj, scalars, subln_g, *, row0, batch, seq, heads, dc, col_q, col_k, col_v, tq):
    hd = 2 * dc
    assert hd == LANES and row0 % seq == 0 and seq % tq == 0
    nq = seq // tq
    rb = row0 // seq
    return pl.pallas_call(
        functools.partial(_diff_kernel, tq=tq, seq=seq, dc=dc, scale=dc ** -0.5),
        grid=(batch, heads, nq),
        in_specs=[pl.BlockSpec(memory_space=pltpu.SMEM),
                  pl.BlockSpec((tq, hd), lambda b, h, i: ((rb + b) * nq + i, col_q // hd + h)),
                  pl.BlockSpec((seq, hd), lambda b, h, i: (rb + b, col_k // hd + h)),
                  pl.BlockSpec((seq, hd), lambda b, h, i: (rb + b, col_v // hd + h)),
                  pl.BlockSpec((1, hd), lambda b, h, i: (0, 0))],
        out_specs=pl.BlockSpec((tq, hd), lambda b, h, i: (b * nq + i, h)),
        out_shape=jax.ShapeDtypeStruct((batch * seq, heads * hd), BF16),
        compiler_params=_cparams("parallel", "parallel", "arbitrary"),
        name="diff_attention",
    )(scalars, proj, proj, proj, subln_g)


def _layer_norm(y, g, b):
    mu = jnp.mean(y, axis=-1, keepdims=True)
    yc = y - mu
    var = jnp.mean(yc * yc, axis=-1, keepdims=True)
    return yc * lax.rsqrt(var + LN_EPS) * g + b


def _merge_kernel(x_ref, g_ref, oa_ref, ob_ref, oc_ref, wa_ref, wb_ref, wc_ref, wo_ref, lg_ref, lb_ref,
                  o_ref, *, d, alpha):
    merged = None
    for idx, (o_br, w_br) in enumerate(((oa_ref, wa_ref), (ob_ref, wb_ref), (oc_ref, wc_ref))):
        gate = jax.nn.sigmoid(g_ref[:, idx * d:(idx + 1) * d].astype(F32))
        term = gate * jnp.dot(o_br[...], w_br[...], preferred_element_type=F32)
        merged = term if merged is None else merged + term
    y = alpha * x_ref[...] + jnp.dot(merged.astype(BF16), wo_ref[...], preferred_element_type=F32)
    o_ref[...] = _layer_norm(y, lg_ref[...], lb_ref[...])


def _merge_out_ln(x, proj, oa, ob, oc, wa, wb, wc, wo, ln_g, ln_b, *, alpha, tm):
    n, d = x.shape
    row = lambda width: pl.BlockSpec((tm, width), lambda i: (i, 0))
    full = lambda arr: _resident(arr.shape, lambda i: (0, 0))
    return pl.pallas_call(
        functools.partial(_merge_kernel, d=d, alpha=alpha),
        grid=(n // tm,),
        in_specs=[row(d), row(3 * d), row(oa.shape[1]), row(ob.shape[1]), row(oc.shape[1]),
                  full(wa), full(wb), full(wc), full(wo), full(ln_g), full(ln_b)],
        out_specs=row(d),
        out_shape=jax.ShapeDtypeStruct((n, d), F32),
        compiler_params=_cparams("parallel"),
        name="merge_out_ln1",
    )(x, proj, oa, ob, oc, wa, wb, wc, wo, ln_g, ln_b)


def _router_kernel(x_ref, w_ref, b_ref, idx_ref, wgt_ref, rank_ref, cnt_ref, carry_ref, *, tt, n_exp):
    @pl.when(pl.program_id(0) == 0)
    def _():
        carry_ref[...] = jnp.zeros_like(carry_ref)

    logits = jnp.dot(x_ref[...], w_ref[...], preferred_element_type=F32,
                     precision=lax.Precision.HIGHEST) + b_ref[...]
    lane = lax.broadcasted_iota(I32, (tt, n_exp), 1).astype(F32)
    work = logits
    vals, idxs, hots = [], [], []
    for _ in range(TOP_K):
        m = jnp.max(work, axis=-1, keepdims=True)
        idx = jnp.min(jnp.where(work == m, lane, float(n_exp)), axis=-1, keepdims=True)
        hot = lane == idx
        vals.append(m)
        idxs.append(idx)
        hots.append(hot)
        work = jnp.where(hot, -jnp.inf, work)
    exps = [jnp.exp(v - vals[0]) for v in vals]
    denom = exps[0]
    for e in exps[1:]:
        denom = denom + e
    onehot = jnp.zeros((tt, n_exp), F32)
    for hot in hots:
        onehot = onehot + hot.astype(F32)
    r_i = lax.broadcasted_iota(I32, (tt, tt), 0)
    c_i = lax.broadcasted_iota(I32, (tt, tt), 1)
    lower = jnp.where(r_i > c_i, 1.0, 0.0).astype(BF16)
    before = jnp.dot(lower, onehot.astype(BF16), preferred_element_type=F32) + carry_ref[...]
    for k in range(TOP_K):
        idx_ref[:, k:k + 1] = idxs[k].astype(I32)
        wgt_ref[:, k:k + 1] = exps[k] / denom
        rank_ref[:, k:k + 1] = jnp.sum(jnp.where(hots[k], before, 0.0), axis=-1, keepdims=True).astype(I32)
    carry_ref[...] = carry_ref[...] + jnp.sum(onehot, axis=0, keepdims=True)
    cnt_ref[...] = carry_ref[...].astype(I32)


def _router(x, w_router, b_router, *, tt):
    n, d = x.shape
    n_exp = w_router.shape[1]
    out4 = lambda dt: jax.ShapeDtypeStruct((n, TOP_K), dt)
    spec4 = pl.BlockSpec((tt, TOP_K), lambda i: (i, 0))
    return pl.pallas_call(
        functools.partial(_router_kernel, tt=tt, n_exp=n_exp),
        grid=(n // tt,),
        in_specs=[pl.BlockSpec((tt, d), lambda i: (i, 0)),
                  _resident((d, n_exp), lambda i: (0, 0)),
                  _resident((1, n_exp), lambda i: (0, 0))],
        out_specs=[spec4, spec4, spec4, pl.BlockSpec((1, n_exp), lambda i: (0, 0))],
        out_shape=[out4(I32), out4(F32), out4(I32), jax.ShapeDtypeStruct((1, n_exp), I32)],
        scratch_shapes=[pltpu.VMEM((1, n_exp), F32)],
        compiler_params=_cparams("arbitrary"),
        name="moe_router",
    )(x, w_router, b_router)


def _gather_kernel(idx_ref, src_ref, o_ref, sem, *, rows_per_step):
    base = pl.program_id(0) * rows_per_step

    def issue(r, carry):
        pltpu.make_async_copy(src_ref.at[pl.ds(idx_ref[r], 1)], o_ref.at[pl.ds(base + r, 1)], sem).start()
        return carry

    def drain(r, carry):
        pltpu.make_async_copy(src_ref.at[pl.ds(0, 1)], o_ref.at[pl.ds(base + r, 1)], sem).wait()
        return carry

    lax.fori_loop(0, rows_per_step, issue, 0, unroll=8)
    lax.fori_loop(0, rows_per_step, drain, 0, unroll=8)


def _gather_rows(src, idx, *, rows_per_step):
    k = idx.shape[0]
    assert k % rows_per_step == 0
    return pl.pallas_call(
        functools.partial(_gather_kernel, rows_per_step=rows_per_step),
        grid=(k // rows_per_step,),
        in_specs=[pl.BlockSpec((rows_per_step,), lambda i: (i,), memory_space=pltpu.SMEM),
                  pl.BlockSpec(memory_space=pl.ANY)],
        out_specs=pl.BlockSpec(memory_space=pl.ANY),
        out_shape=jax.ShapeDtypeStruct((k, src.shape[1]), src.dtype),
        scratch_shapes=[pltpu.SemaphoreType.DMA],
        compiler_params=_cparams("arbitrary"),
        name="gather_rows",
    )(idx, src)


def _expert_kernel(te_ref, tv_ref, x_ref, w1g_ref, w1l_ref, b1g_ref, b1l_ref, w2_ref, b2_ref, o_ref,
                   xb_ref, acc_ref):
    i = pl.program_id(0)
    f = pl.program_id(1)
    last = pl.num_programs(1) - 1
    live = tv_ref[i] > 0

    @pl.when(live & (f == 0))
    def _():
        xb_ref[...] = x_ref[...].astype(BF16)
        acc_ref[...] = jnp.zeros_like(acc_ref)

    @pl.when(live)
    def _():
        xb = xb_ref[...]
        hg = jnp.dot(xb, w1g_ref[...], preferred_element_type=F32) + b1g_ref[...]
        hl = jnp.dot(xb, w1l_ref[...], preferred_element_type=F32) + b1l_ref[...]
        glu = jnp.minimum(hg, SWIGLU_LIMIT)
        lin = jnp.clip(hl, -SWIGLU_LIMIT, SWIGLU_LIMIT)
        act = glu * jax.nn.sigmoid(SWIGLU_ALPHA * glu) * (lin + 1.0)
        acc_ref[...] += jnp.dot(act.astype(BF16), w2_ref[...], preferred_element_type=F32)

    @pl.when(live & (f == last))
    def _():
        o_ref[...] = acc_ref[...] + b2_ref[...]

    @pl.when(jnp.logical_not(live) & (f == last))
    def _():
        o_ref[...] = jnp.zeros_like(o_ref)


def _expert_mlp(xrows, tile_expert, tile_valid, w1g, w1l, b1g, b1l, w2, b2, *, tm, tf):
    r, d = xrows.shape
    dff = w1g.shape[2]
    nf = dff // tf
    fidx = lambda f, tv, i: jnp.where(tv[i] > 0, f, nf - 1)
    grid_spec = pltpu.PrefetchScalarGridSpec(
        num_scalar_prefetch=2,
        grid=(r // tm, nf),
        in_specs=[pl.BlockSpec((tm, d), lambda i, f, te, tv: (i, 0)),
                  pl.BlockSpec((None, d, tf), lambda i, f, te, tv: (te[i], 0, fidx(f, tv, i))),
                  pl.BlockSpec((None, d, tf), lambda i, f, te, tv: (te[i], 0, fidx(f, tv, i))),
                  pl.BlockSpec((None, 1, tf), lambda i, f, te, tv: (te[i], 0, fidx(f, tv, i))),
                  pl.BlockSpec((None, 1, tf), lambda i, f, te, tv: (te[i], 0, fidx(f, tv, i))),
                  pl.BlockSpec((None, tf, d), lambda i, f, te, tv: (te[i], fidx(f, tv, i), 0)),
                  pl.BlockSpec((None, 1, d), lambda i, f, te, tv: (te[i], 0, 0))],
        out_specs=pl.BlockSpec((tm, d), lambda i, f, te, tv: (i, 0)),
        scratch_shapes=[pltpu.VMEM((tm, d), BF16), pltpu.VMEM((tm, d), F32)],
    )
    return pl.pallas_call(
        _expert_kernel,
        grid_spec=grid_spec,
        out_shape=jax.ShapeDtypeStruct((r, d), F32),
        compiler_params=_cparams("parallel", "arbitrary"),
        name="expert_mlp",
    )(tile_expert, tile_valid, xrows, w1g, w1l, b1g, b1l, w2, b2)


def _combine_kernel(x_ref, y_ref, w_ref, lg_ref, lb_ref, o_ref, *, alpha):
    y = alpha * x_ref[...]
    for k in range(TOP_K):
        y = y + w_ref[:, k:k + 1] * y_ref[k]
    o_ref[...] = _layer_norm(y, lg_ref[...], lb_ref[...])


def _combine_ln(x, ygath, top_w, ln_g, ln_b, *, alpha, tt):
    n, d = x.shape
    return pl.pallas_call(
        functools.partial(_combine_kernel, alpha=alpha),
        grid=(n // tt,),
        in_specs=[pl.BlockSpec((tt, d), lambda i: (i, 0)),
                  pl.BlockSpec((TOP_K, tt, d), lambda i: (0, i, 0)),
                  pl.BlockSpec((tt, TOP_K), lambda i: (i, 0)),
                  _resident((1, d), lambda i: (0, 0)),
                  _resident((1, d), lambda i: (0, 0))],
        out_specs=pl.BlockSpec((tt, d), lambda i: (i, 0)),
        out_shape=jax.ShapeDtypeStruct((n, d), F32),
        compiler_params=_cparams("parallel"),
        name="combine_ln2",
    )(x, ygath, top_w, ln_g, ln_b)


def _moe_layout(top_i, rank, counts, *, tm):
    n, k = top_i.shape
    n_exp = counts.shape[0]
    n_tiles = (n * k) // tm + n_exp
    padded = (counts + tm - 1) // tm * tm
    pends = jnp.cumsum(padded)
    pstarts = pends - padded
    dest = pstarts[top_i] + rank
    tile_start = jnp.arange(n_tiles, dtype=I32) * tm
    tile_expert = jnp.clip(jnp.searchsorted(pends, tile_start, side="right"), 0, n_exp - 1).astype(I32)
    tile_valid = jnp.clip(counts[tile_expert] - (tile_start - pstarts[tile_expert]), 0, tm).astype(I32)
    tok = jnp.broadcast_to(jnp.arange(n, dtype=I32)[:, None], (n, k))
    row_tok = jnp.zeros((n_tiles * tm,), I32).at[dest.reshape(-1)].set(tok.reshape(-1))
    return dest, row_tok, tile_expert, tile_valid


def _moe(x, p, *, tm, tf, tt_router, gather_step):
    top_i, top_w, rank, counts = _router(x, p["w_router"], p["b_router"], tt=tt_router)
    dest, row_tok, tile_expert, tile_valid = _moe_layout(top_i, rank, counts[0], tm=tm)
    xrows = _gather_rows(x, row_tok, rows_per_step=gather_step)
    yrows = _expert_mlp(xrows, tile_expert, tile_valid, p["w1g"], p["w1l"], p["b1g"], p["b1l"],
                        p["w2"], p["b2"], tm=tm, tf=tf)
    n, d = x.shape
    ygath = _gather_rows(yrows, dest.T.reshape(-1), rows_per_step=gather_step).reshape(TOP_K, n, d)
    return ygath, top_w


def _prepare_params(w_in, rpb_a, conv_w, conv_b, lru_wa, lru_ba, lru_wx, lru_bx, lru_lambda, lam_q1, lam_k1,
                    lam_q2, lam_k2, subln_g, w_proj_a, w_proj_b, w_proj_c, w_out, ln1_g, ln1_b, w_router,
                    b_router, w1, b1, w2, b2, ln2_g, ln2_b, *, d_model, kh_by_rows, diff_heads):
    depth = w_in.shape[0]
    gate0 = w_in.shape[2] - 3 * d_model
    layers = []
    for l in range(depth):
        lam_init = 0.8 - 0.6 * math.exp(-0.3 * l)
        lam = (jnp.exp(jnp.sum(lam_q1[l].astype(F32) * lam_k1[l].astype(F32)))
               - jnp.exp(jnp.sum(lam_q2[l].astype(F32) * lam_k2[l].astype(F32))) + lam_init)
        slopes = 2.0 ** (-8.0 * (jnp.arange(diff_heads, dtype=F32) + 1.0) / diff_heads)
        layers.append(dict(
            w_in=jnp.concatenate([w_in[l, :, gate0:], w_in[l, :, :gate0]], axis=1).astype(BF16),
            na_bias={kh: _na_bias_table(rpb_a[l], kh) for kh in kh_by_rows},
            conv_w=conv_w[l], conv_b=conv_b[l][None, :],
            lru_wa=lru_wa[l].astype(BF16), lru_ba=lru_ba[l], lru_wx=lru_wx[l].astype(BF16), lru_bx=lru_bx[l],
            lru_log_scale=-LRU_C * jax.nn.softplus(-lru_lambda[l].astype(F32)),
            diff_scalars=jnp.concatenate([jnp.stack([lam, jnp.asarray(1.0 - lam_init, F32)]), slopes]).astype(F32),
            subln_g=subln_g[l][None, :],
            w_proj_a=w_proj_a[l].astype(BF16), w_proj_b=w_proj_b[l].astype(BF16),
            w_proj_c=w_proj_c[l].astype(BF16), w_out=w_out[l].astype(BF16),
            ln1_g=ln1_g[l][None, :], ln1_b=ln1_b[l][None, :],
            w_router=w_router[l], b_router=b_router[l][None, :],
            w1g=w1[l, :, :, 0::2].astype(BF16), w1l=w1[l, :, :, 1::2].astype(BF16),
            b1g=b1[l, :, None, 0::2], b1l=b1[l, :, None, 1::2],
            w2=w2[l].astype(BF16), b2=b2[l][:, None, :],
            ln2_g=ln2_g[l][None, :], ln2_b=ln2_b[l][None, :],
        ))
    return layers


def _encoder_layer(x, p, groups, *, alpha, cfg):
    d = x.shape[1]
    proj = _in_proj(x, p["w_in"], cfg["proj_tm"], cfg["proj_tn"])
    na_w = cfg["na_heads"] * LANES
    rnn_w = p["lru_ba"].shape[1]
    dq = cfg["diff_heads"] * 2 * cfg["diff_dc"]
    c_qa = 3 * d
    c_xb = c_qa + 3 * na_w
    c_qc = c_xb + 2 * rnn_w
    oa, ob, oc = [], [], []
    for row0, batch, seq in groups:
        kh = min(NA_KH_MAX, seq // GRID_W)
        oa.append(_na_attention(proj, p["na_bias"][kh], row0=row0, batch=batch, seq=seq, heads=cfg["na_heads"],
                                col_q=c_qa, col_k=c_qa + na_w, col_v=c_qa + 2 * na_w))
        ob.append(_rglru(proj, p["conv_w"], p["conv_b"], p["lru_wa"], p["lru_ba"], p["lru_wx"], p["lru_bx"],
                         p["lru_log_scale"], row0=row0, batch=batch, seq=seq, col_x=c_xb, col_y=c_xb + rnn_w))
        oc.append(_diff_attention(proj, p["diff_scalars"], p["subln_g"], row0=row0, batch=batch, seq=seq,
                                  heads=cfg["diff_heads"], dc=cfg["diff_dc"], col_q=c_qc, col_k=c_qc + dq,
                                  col_v=c_qc + 2 * dq, tq=cfg["diff_tq"]))
    oa, ob, oc = (jnp.concatenate(t, axis=0) for t in (oa, ob, oc))
    x1 = _merge_out_ln(x, proj, oa, ob, oc, p["w_proj_a"], p["w_proj_b"], p["w_proj_c"], p["w_out"],
                       p["ln1_g"], p["ln1_b"], alpha=alpha, tm=cfg["merge_tm"])
    ygath, top_w = _moe(x1, p, tm=cfg["moe_tm"], tf=cfg["moe_tf"], tt_router=cfg["router_tt"],
                        gather_step=cfg["gather_step"])
    return _combine_ln(x1, ygath, top_w, p["ln2_g"], p["ln2_b"], alpha=alpha, tt=cfg["combine_tt"])


DEFAULT_CFG = dict(proj_tm=1024, proj_tn=1024, na_heads=4, diff_heads=4, diff_dc=64, diff_tq=256, merge_tm=256,
                   moe_tm=512, moe_tf=512, router_tt=512, combine_tt=256, gather_step=2048)


def _trunk(xs, params, cfg):
    d = xs[0].shape[-1]
    groups, row0 = [], 0
    for x in xs:
        groups.append((row0, x.shape[0], x.shape[1]))
        row0 += x.shape[0] * x.shape[1]
    kh_by_rows = sorted({min(NA_KH_MAX, s // GRID_W) for _, _, s in groups})
    layers = _prepare_params(*params, d_model=d, kh_by_rows=kh_by_rows, diff_heads=cfg["diff_heads"])
    alpha = (2 * len(layers)) ** 0.25
    x = jnp.concatenate([x.reshape(-1, d) for x in xs], axis=0)
    for p in layers:
        x = _encoder_layer(x, p, tuple(groups), alpha=alpha, cfg=cfg)
    outs = []
    for (r0, b, s), xin in zip(groups, xs):
        outs.append(x[r0:r0 + b * s].reshape(xin.shape))
    return tuple(outs)


def kernel(x_prompt, x_sample, w_in, rpb_a, conv_w, conv_b, lru_wa, lru_ba, lru_wx, lru_bx, lru_lambda, lam_q1,
           lam_k1, lam_q2, lam_k2, subln_g, w_proj_a, w_proj_b, w_proj_c, w_out, ln1_g, ln1_b, w_router,
           b_router, w1, b1, w2, b2, ln2_g, ln2_b):
    params = (w_in, rpb_a, conv_w, conv_b, lru_wa, lru_ba, lru_wx, lru_bx, lru_lambda, lam_q1, lam_k1, lam_q2,
              lam_k2, subln_g, w_proj_a, w_proj_b, w_proj_c, w_out, ln1_g, ln1_b, w_router, b_router, w1, b1,
              w2, b2, ln2_g, ln2_b)
    return _trunk((x_prompt, x_sample), params, DEFAULT_CFG)
```

```python
import functools
import math

import jax
import jax.numpy as jnp
from jax import lax
from jax.experimental import pallas as pl
from jax.experimental.pallas import tpu as pltpu

F32 = jnp.float32
BF16 = jnp.bfloat16
I32 = jnp.int32

GRID_W = 64
NA_KH_MAX = 8
NA_KW = 16
LRU_C = 8.0
TOP_K = 4
SWIGLU_LIMIT = 7.0
SWIGLU_ALPHA = 1.702
LN_EPS = 1e-5
NEG_INF = -1e30

LANES = 128
SUBLANES = 8
VMEM_LIMIT = 56 * 1024 * 1024


def _cparams(*sem):
    return pltpu.CompilerParams(dimension_semantics=sem, vmem_limit_bytes=VMEM_LIMIT)


def _resident(block_shape, index_map):
    return pl.BlockSpec(block_shape, index_map, pipeline_mode=pl.Buffered(1))


def _in_proj_kernel(x_ref, w_ref, o_ref, xb_ref):
    @pl.when(pl.program_id(1) == 0)
    def _():
        xb_ref[...] = x_ref[...].astype(BF16)

    o_ref[...] = jnp.dot(xb_ref[...], w_ref[...], preferred_element_type=F32).astype(o_ref.dtype)


def _in_proj(x, w, tm, tn):
    n, d = x.shape
    width = w.shape[1]
    return pl.pallas_call(
        _in_proj_kernel,
        grid=(n // tm, width // tn),
        in_specs=[pl.BlockSpec((tm, d), lambda i, j: (i, 0)),
                  pl.BlockSpec((d, tn), lambda i, j: (0, j))],
        out_specs=pl.BlockSpec((tm, tn), lambda i, j: (i, j)),
        out_shape=jax.ShapeDtypeStruct((n, width), BF16),
        scratch_shapes=[pltpu.VMEM((tm, d), BF16)],
        compiler_params=_cparams("parallel", "arbitrary"),
        name="in_proj",
    )(x, w)


def _na_bias_table(rpb, kh):
    d = jnp.arange(kh)[:, None, None, None]
    c = jnp.arange(GRID_W)[None, :, None, None]
    r = jnp.arange(kh)[None, None, :, None]
    kc = jnp.arange(GRID_W)[None, None, None, :]
    row_off = r - d + NA_KH_MAX - 1
    col_off = jnp.clip(kc - c + NA_KW - 1, 0, 2 * NA_KW - 2)
    win = jnp.clip(c - NA_KW // 2, 0, GRID_W - NA_KW)
    ok = (kc >= win) & (kc < win + NA_KW)
    row_off, col_off, ok = jnp.broadcast_arrays(row_off, col_off, ok)
    bias = rpb.astype(F32)[:, row_off, col_off]
    bias = jnp.where(ok[None], bias, NEG_INF)
    return bias.reshape(rpb.shape[0], kh, GRID_W, kh * GRID_W)


def _na_kernel(q_ref, k_ref, v_ref, bias_ref, o_ref, *, rows, kh, scale):
    def row_body(i, carry):
        r0 = jnp.clip(i - kh // 2, 0, rows - kh)
        q = q_ref[pl.ds(pl.multiple_of(i * GRID_W, GRID_W), GRID_W), :]
        koff = pl.multiple_of(r0 * GRID_W, GRID_W)
        kw = k_ref[pl.ds(koff, kh * GRID_W), :]
        vw = v_ref[pl.ds(koff, kh * GRID_W), :]
        s = lax.dot_general(q, kw, (((1,), (1,)), ((), ())), preferred_element_type=F32)
        s = s * scale + bias_ref[i - r0]
        m = jnp.max(s, axis=-1, keepdims=True)
        p = jnp.exp(s - m)
        l = jnp.sum(p, axis=-1, keepdims=True)
        o = jnp.dot(p.astype(BF16), vw, preferred_element_type=F32) / l
        o_ref[pl.ds(pl.multiple_of(i * GRID_W, GRID_W), GRID_W), :] = o.astype(o_ref.dtype)
        return carry

    lax.fori_loop(0, rows, row_body, 0)


def _na_attention(proj, bias, *, row0, batch, seq, heads, col_q, col_k, col_v):
    hd = LANES
    rows = seq // GRID_W
    kh = bias.shape[1]
    rb = row0 // seq
    assert row0 % seq == 0 and rows >= kh
    spec = lambda col: pl.BlockSpec((seq, hd), lambda b, h: (rb + b, col // hd + h))
    return pl.pallas_call(
        functools.partial(_na_kernel, rows=rows, kh=kh, scale=hd ** -0.5),
        grid=(batch, heads),
        in_specs=[spec(col_q), spec(col_k), spec(col_v),
                  pl.BlockSpec((None, kh, GRID_W, kh * GRID_W), lambda b, h: (h, 0, 0, 0))],
        out_specs=pl.BlockSpec((seq, hd), lambda b, h: (b, h)),
        out_shape=jax.ShapeDtypeStruct((batch * seq, heads * hd), BF16),
        compiler_params=_cparams("parallel", "parallel"),
        name="na_attention",
    )(proj, proj, proj, bias)


def _lru_kernel(x_ref, y_ref, cw_ref, cb_ref, wa_ref, ba_ref, wx_ref, bx_ref, ls_ref, o_ref,
                a0_ref, b0_ref, a1_ref, b1_ref, *, seq):
    chunk = seq // SUBLANES
    x = x_ref[...].astype(F32)
    t_idx = lax.broadcasted_iota(I32, x.shape, 0)
    xc = (cw_ref[0:1, :] * jnp.where(t_idx >= 2, pltpu.roll(x, 2, 0), 0.0)
          + cw_ref[1:2, :] * jnp.where(t_idx >= 1, pltpu.roll(x, 1, 0), 0.0)
          + cw_ref[2:3, :] * x
          + cw_ref[3:4, :] * jnp.where(t_idx < seq - 1, pltpu.roll(x, seq - 1, 0), 0.0)
          + cb_ref[...])
    xb = xc.astype(BF16)
    for d, (a_ref, b_ref) in enumerate(((a0_ref, b0_ref), (a1_ref, b1_ref))):
        r = jax.nn.sigmoid(jnp.dot(xb, wa_ref[d], preferred_element_type=F32) + ba_ref[d:d + 1, :])
        ig = jax.nn.sigmoid(jnp.dot(xb, wx_ref[d], preferred_element_type=F32) + bx_ref[d:d + 1, :])
        a = jnp.exp(ls_ref[d:d + 1, :] * r)
        a_ref[...] = a
        b_ref[...] = jnp.sqrt(1.0 - a * a) * (ig * xc)

    def scan_body(t, carry):
        hf, pf, hb, pb = carry
        tb = chunk - 1 - t
        af = a0_ref[pl.ds(t, SUBLANES, stride=chunk), :]
        bf = b0_ref[pl.ds(t, SUBLANES, stride=chunk), :]
        ab = a1_ref[pl.ds(tb, SUBLANES, stride=chunk), :]
        bb = b1_ref[pl.ds(tb, SUBLANES, stride=chunk), :]
        hf = af * hf + bf
        pf = pf * af
        hb = ab * hb + bb
        pb = pb * ab
        b0_ref[pl.ds(t, SUBLANES, stride=chunk), :] = hf
        a0_ref[pl.ds(t, SUBLANES, stride=chunk), :] = pf
        b1_ref[pl.ds(tb, SUBLANES, stride=chunk), :] = hb
        a1_ref[pl.ds(tb, SUBLANES, stride=chunk), :] = pb
        return hf, pf, hb, pb

    zeros = jnp.zeros((SUBLANES, LANES), F32)
    ones = jnp.ones((SUBLANES, LANES), F32)
    hf, pf, hb, pb = lax.fori_loop(0, chunk, scan_body, (zeros, ones, zeros, ones), unroll=8)

    cf = [jnp.zeros((1, LANES), F32)]
    for j in range(SUBLANES - 1):
        cf.append(pf[j:j + 1] * cf[j] + hf[j:j + 1])
    cb = [jnp.zeros((1, LANES), F32)]
    for j in range(SUBLANES - 1, 0, -1):
        cb.append(pb[j:j + 1] * cb[-1] + hb[j:j + 1])
    cb = cb[::-1]
    for j in range(SUBLANES):
        sl = pl.ds(j * chunk, chunk)
        h = (b0_ref[sl, :] + a0_ref[sl, :] * cf[j]) + (b1_ref[sl, :] + a1_ref[sl, :] * cb[j])
        y = y_ref[sl, :].astype(F32)
        o_ref[sl, :] = (h * jax.nn.gelu(y, approximate=True)).astype(o_ref.dtype)


def _rglru(proj, conv_w, conv_b, wa, ba, wx, bx, log_scale, *, row0, batch, seq, col_x, col_y):
    nb = wa.shape[1]
    bw = wa.shape[2]
    assert bw == LANES and row0 % seq == 0 and seq % (SUBLANES * SUBLANES) == 0
    rb = row0 // seq
    cvec = lambda rows_: pl.BlockSpec((rows_, bw), lambda b, n: (0, n))
    wspec = pl.BlockSpec((2, None, bw, bw), lambda b, n: (0, n, 0, 0))
    return pl.pallas_call(
        functools.partial(_lru_kernel, seq=seq),
        grid=(batch, nb),
        in_specs=[pl.BlockSpec((seq, bw), lambda b, n: (rb + b, col_x // bw + n)),
                  pl.BlockSpec((seq, bw), lambda b, n: (rb + b, col_y // bw + n)),
                  cvec(conv_w.shape[0]), cvec(1), wspec, cvec(2), wspec, cvec(2), cvec(2)],
        out_specs=pl.BlockSpec((seq, bw), lambda b, n: (b, n)),
        out_shape=jax.ShapeDtypeStruct((batch * seq, nb * bw), BF16),
        scratch_shapes=[pltpu.VMEM((seq, bw), F32)] * 4,
        compiler_params=_cparams("parallel", "parallel"),
        name="rglru",
    )(proj, proj, conv_w, conv_b, wa, ba, wx, bx, log_scale)


def _diff_kernel(sc_ref, q_ref, k_ref, v_ref, g_ref, o_ref, *, tq, seq, dc, scale):
    h = pl.program_id(1)
    lam = sc_ref[0]
    out_scale = sc_ref[1]
    slope = sc_ref[2 + h]
    q = q_ref[...]
    k = k_ref[...]
    v = v_ref[...]
    lane = lax.broadcasted_iota(I32, q.shape, 1)
    qpos = pl.program_id(2) * tq + lax.broadcasted_iota(I32, (tq, 1), 0)
    kpos = lax.broadcasted_iota(I32, (1, seq), 1)
    bias = jnp.abs(qpos - kpos).astype(F32) * (-slope)
    outs = []
    for m_idx in range(2):
        in_map = (lane < dc) if m_idx == 0 else (lane >= dc)
        qm = jnp.where(in_map, q, jnp.zeros_like(q))
        s = lax.dot_general(qm, k, (((1,), (1,)), ((), ())), preferred_element_type=F32)
        s = s * scale + bias
        mx = jnp.max(s, axis=-1, keepdims=True)
        p = jnp.exp(s - mx)
        l = jnp.sum(p, axis=-1, keepdims=True)
        outs.append(jnp.dot(p.astype(BF16), v, preferred_element_type=F32) / l)
    o = outs[0] - lam * outs[1]
    o = o * lax.rsqrt(jnp.mean(o * o, axis=-1, keepdims=True) + LN_EPS)
    o_ref[...] = (o * g_ref[...] * out_scale).astype(o_ref.dtype)


def _diff_attention(proj, scalars, subln_g, *, row0, batch, seq, heads, dc, col_q, col_k, col_v, tq):
    hd = 2 * dc
    assert hd == LANES and row0 % seq == 0 and seq % tq == 0
    nq = seq // tq
    rb = row0 // seq
    return pl.pallas_call(
        functools.partial(_diff_kernel, tq=tq, seq=seq, dc=dc, scale=dc ** -0.5),
        grid=(batch, heads, nq),
        in_specs=[pl.BlockSpec(memory_space=pltpu.SMEM),
                  pl.BlockSpec((tq, hd), lambda b, h, i: ((rb + b) * nq + i, col_q // hd + h)),
                  pl.BlockSpec((seq, hd), lambda b, h, i: (rb + b, col_k // hd + h)),
                  pl.BlockSpec((seq, hd), lambda b, h, i: (rb + b, col_v // hd + h)),
                  pl.BlockSpec((1, hd), lambda b, h, i: (0, 0))],
        out_specs=pl.BlockSpec((tq, hd), lambda b, h, i: (b * nq + i, h)),
        out_shape=jax.ShapeDtypeStruct((batch * seq, heads * hd), BF16),
        compiler_params=_cparams("parallel", "parallel", "arbitrary"),
        name="diff_attention",
    )(scalars, proj, proj, proj, subln_g)


def _layer_norm(y, g, b):
    mu = jnp.mean(y, axis=-1, keepdims=True)
    yc = y - mu
    var = jnp.mean(yc * yc, axis=-1, keepdims=True)
    return yc * lax.rsqrt(var + LN_EPS) * g + b


def _packed_rows(d):
    return d // (2 * LANES)


def _store_packed(ref, y):
    m, d = y.shape
    pr = _packed_rows(d)
    lo = lax.bitcast_convert_type(y[:, :d // 2].astype(BF16).astype(F32), jnp.uint32)
    hi = lax.bitcast_convert_type(y[:, d // 2:].astype(BF16).astype(F32), jnp.uint32)
    word = hi | (lo >> 16)
    for c in range(pr):
        ref[pl.ds(c, m, stride=pr), :] = word[:, c * LANES:(c + 1) * LANES]


def _load_packed(ref, c, m, pr, row0=0):
    word = ref[pl.ds(row0 * pr + c, m, stride=pr), :]
    lo = lax.bitcast_convert_type(word << 16, F32)
    hi = lax.bitcast_convert_type(word & jnp.uint32(0xFFFF0000), F32)
    return lo, hi


def _merge_kernel(x_ref, g_ref, oa_ref, ob_ref, oc_ref, wa_ref, wb_ref, wc_ref, wo_ref, lg_ref, lb_ref,
                  o_ref, op_ref, *, d, alpha):
    merged = None
    for idx, (o_br, w_br) in enumerate(((oa_ref, wa_ref), (ob_ref, wb_ref), (oc_ref, wc_ref))):
        gate = jax.nn.sigmoid(g_ref[:, idx * d:(idx + 1) * d].astype(F32))
        term = gate * jnp.dot(o_br[...], w_br[...], preferred_element_type=F32)
        merged = term if merged is None else merged + term
    y = alpha * x_ref[...] + jnp.dot(merged.astype(BF16), wo_ref[...], preferred_element_type=F32)
    y = _layer_norm(y, lg_ref[...], lb_ref[...])
    o_ref[...] = y
    _store_packed(op_ref, y)


def _merge_out_ln(x, proj, oa, ob, oc, wa, wb, wc, wo, ln_g, ln_b, *, alpha, tm):
    n, d = x.shape
    pr = _packed_rows(d)
    row = lambda width: pl.BlockSpec((tm, width), lambda i: (i, 0))
    full = lambda arr: _resident(arr.shape, lambda i: (0, 0))
    return pl.pallas_call(
        functools.partial(_merge_kernel, d=d, alpha=alpha),
        grid=(n // tm,),
        in_specs=[row(d), row(3 * d), row(oa.shape[1]), row(ob.shape[1]), row(oc.shape[1]),
                  full(wa), full(wb), full(wc), full(wo), full(ln_g), full(ln_b)],
        out_specs=[row(d), pl.BlockSpec((tm * pr, LANES), lambda i: (i, 0))],
        out_shape=[jax.ShapeDtypeStruct((n, d), F32), jax.ShapeDtypeStruct((n * pr, LANES), jnp.uint32)],
        compiler_params=_cparams("parallel"),
        name="merge_out_ln1",
    )(x, proj, oa, ob, oc, wa, wb, wc, wo, ln_g, ln_b)


def _router_kernel(x_ref, w_ref, b_ref, idx_ref, wgt_ref, rank_ref, cnt_ref, carry_ref, *, tt, n_exp):
    @pl.when(pl.program_id(0) == 0)
    def _():
        carry_ref[...] = jnp.zeros_like(carry_ref)

    logits = jnp.dot(x_ref[...], w_ref[...], preferred_element_type=F32,
                     precision=lax.Precision.HIGHEST) + b_ref[...]
    lane = lax.broadcasted_iota(I32, (tt, n_exp), 1).astype(F32)
    work = logits
    vals, idxs, hots = [], [], []
    for _ in range(TOP_K):
        m = jnp.max(work, axis=-1, keepdims=True)
        idx = jnp.min(jnp.where(work == m, lane, float(n_exp)), axis=-1, keepdims=True)
        hot = lane == idx
        vals.append(m)
        idxs.append(idx)
        hots.append(hot)
        work = jnp.where(hot, -jnp.inf, work)
    exps = [jnp.exp(v - vals[0]) for v in vals]
    denom = exps[0]
    for e in exps[1:]:
        denom = denom + e
    onehot = jnp.zeros((tt, n_exp), F32)
    for hot in hots:
        onehot = onehot + hot.astype(F32)
    r_i = lax.broadcasted_iota(I32, (tt, tt), 0)
    c_i = lax.broadcasted_iota(I32, (tt, tt), 1)
    lower = jnp.where(r_i > c_i, 1.0, 0.0).astype(BF16)
    before = jnp.dot(lower, onehot.astype(BF16), preferred_element_type=F32) + carry_ref[...]
    for k in range(TOP_K):
        idx_ref[:, k:k + 1] = idxs[k].astype(I32)
        wgt_ref[:, k:k + 1] = exps[k] / denom
        rank_ref[:, k:k + 1] = jnp.sum(jnp.where(hots[k], before, 0.0), axis=-1, keepdims=True).astype(I32)
    carry_ref[...] = carry_ref[...] + jnp.sum(onehot, axis=0, keepdims=True)
    cnt_ref[...] = carry_ref[...].astype(I32)


def _router(x, w_router, b_router, *, tt):
    n, d = x.shape
    n_exp = w_router.shape[1]
    out4 = lambda dt: jax.ShapeDtypeStruct((n, TOP_K), dt)
    spec4 = pl.BlockSpec((tt, TOP_K), lambda i: (i, 0))
    return pl.pallas_call(
        functools.partial(_router_kernel, tt=tt, n_exp=n_exp),
        grid=(n // tt,),
        in_specs=[pl.BlockSpec((tt, d), lambda i: (i, 0)),
                  _resident((d, n_exp), lambda i: (0, 0)),
                  _resident((1, n_exp), lambda i: (0, 0))],
        out_specs=[spec4, spec4, spec4, pl.BlockSpec((1, n_exp), lambda i: (0, 0))],
        out_shape=[out4(I32), out4(F32), out4(I32), jax.ShapeDtypeStruct((1, n_exp), I32)],
        scratch_shapes=[pltpu.VMEM((1, n_exp), F32)],
        compiler_params=_cparams("arbitrary"),
        name="moe_router",
    )(x, w_router, b_router)


def _w1_prep_kernel(w_ref, o_ref):
    grp = 2 * LANES
    r = lax.broadcasted_iota(I32, (grp, grp), 0)
    c = lax.broadcasted_iota(I32, (grp, grp), 1)
    src = jnp.where(c < LANES, 2 * c, 2 * (c - LANES) + 1)
    perm = jnp.where(r == src, 1.0, 0.0).astype(BF16)
    for g in range(w_ref.shape[1] // grp):
        blk = w_ref[:, g * grp:(g + 1) * grp].astype(BF16)
        o_ref[:, g * grp:(g + 1) * grp] = jnp.dot(blk, perm, preferred_element_type=F32).astype(BF16)


def _w1_prep(w1, *, td, tc):
    n_exp, d, two_f = w1.shape
    spec = pl.BlockSpec((None, td, tc), lambda e, i, j: (e, i, j))
    return pl.pallas_call(
        _w1_prep_kernel,
        grid=(n_exp, d // td, two_f // tc),
        in_specs=[spec],
        out_specs=spec,
        out_shape=jax.ShapeDtypeStruct(w1.shape, BF16),
        compiler_params=_cparams("parallel", "parallel", "parallel"),
        name="w1_prep",
    )(w1)


def _expert_kernel(te_ref, tv_ref, rt_ref, rtn_ref, xp_ref, w1_ref, b1g_ref, b1l_ref, w2_ref, b2_ref, o_ref,
                   xg_ref, sem_ref, xb_ref, acc_ref, *, tm, tf, d):
    i = pl.program_id(0)
    f = pl.program_id(1)
    n_tiles = pl.num_programs(0)
    last = pl.num_programs(1) - 1
    pr = _packed_rows(d)
    slot = i % 2
    live = tv_ref[i] > 0
    nxt = jnp.minimum(i + 1, n_tiles - 1)

    def start_gather(tok_ref, slot_):
        def body(r, carry):
            src = xp_ref.at[pl.ds(pl.multiple_of(tok_ref[r] * pr, pr), pr)]
            dst = xg_ref.at[slot_, pl.ds(pl.multiple_of(r * pr, pr), pr)]
            pltpu.make_async_copy(src, dst, sem_ref.at[slot_]).start()
            return carry
        lax.fori_loop(0, tm, body, 0, unroll=8)

    @pl.when((i == 0) & (f == 0) & live)
    def _():
        start_gather(rt_ref, 0)

    @pl.when(live & (f == 0))
    def _():
        pltpu.make_async_copy(xg_ref.at[slot], xg_ref.at[slot], sem_ref.at[slot]).wait()
        for c in range(pr):
            lo, hi = _load_packed(xg_ref.at[slot], c, tm, pr)
            xb_ref[:, c * LANES:(c + 1) * LANES] = lo.astype(BF16)
            xb_ref[:, d // 2 + c * LANES:d // 2 + (c + 1) * LANES] = hi.astype(BF16)
        acc_ref[...] = jnp.zeros_like(acc_ref)

    @pl.when((f == 0) & (i + 1 < n_tiles) & (tv_ref[nxt] > 0))
    def _():
        start_gather(rtn_ref, 1 - slot)

    @pl.when(live)
    def _():
        h = jnp.dot(xb_ref[...], w1_ref[...], preferred_element_type=F32)
        acts = []
        for j in range(tf // LANES):
            hg = h[:, 2 * j * LANES:(2 * j + 1) * LANES] + b1g_ref[:, j * LANES:(j + 1) * LANES]
            hl = h[:, (2 * j + 1) * LANES:(2 * j + 2) * LANES] + b1l_ref[:, j * LANES:(j + 1) * LANES]
            glu = jnp.minimum(hg, SWIGLU_LIMIT)
            lin = jnp.clip(hl, -SWIGLU_LIMIT, SWIGLU_LIMIT)
            acts.append((glu * jax.nn.sigmoid(SWIGLU_ALPHA * glu) * (lin + 1.0)).astype(BF16))
        act = jnp.concatenate(acts, axis=1)
        acc_ref[...] += jnp.dot(act, w2_ref[...], preferred_element_type=F32)

    @pl.when(live & (f == last))
    def _():
        _store_packed(o_ref, acc_ref[...] + b2_ref[...])

    @pl.when(jnp.logical_not(live) & (f == last))
    def _():
        o_ref[...] = jnp.zeros_like(o_ref)


def _expert_mlp(xp, row_tok, tile_expert, tile_valid, w1p, b1g, b1l, w2, b2, *, tm, tf):
    d = w2.shape[2]
    pr = _packed_rows(d)
    n_tiles = row_tok.shape[0] // tm
    nf = w2.shape[1] // tf
    fidx = lambda f, tv, i: jnp.where(tv[i] > 0, f, nf - 1)
    grid_spec = pltpu.PrefetchScalarGridSpec(
        num_scalar_prefetch=2,
        grid=(n_tiles, nf),
        in_specs=[pl.BlockSpec((tm,), lambda i, f, te, tv: (i,), memory_space=pltpu.SMEM),
                  pl.BlockSpec((tm,), lambda i, f, te, tv: (jnp.minimum(i + 1, n_tiles - 1),),
                               memory_space=pltpu.SMEM),
                  pl.BlockSpec(memory_space=pl.ANY),
                  pl.BlockSpec((None, d, 2 * tf), lambda i, f, te, tv: (te[i], 0, fidx(f, tv, i))),
                  pl.BlockSpec((None, 1, tf), lambda i, f, te, tv: (te[i], 0, fidx(f, tv, i))),
                  pl.BlockSpec((None, 1, tf), lambda i, f, te, tv: (te[i], 0, fidx(f, tv, i))),
                  pl.BlockSpec((None, tf, d), lambda i, f, te, tv: (te[i], fidx(f, tv, i), 0)),
                  pl.BlockSpec((None, 1, d), lambda i, f, te, tv: (te[i], 0, 0))],
        out_specs=pl.BlockSpec((tm * pr, LANES), lambda i, f, te, tv: (i, 0)),
        scratch_shapes=[pltpu.VMEM((2, tm * pr, LANES), jnp.uint32), pltpu.SemaphoreType.DMA((2,)),
                        pltpu.VMEM((tm, d), BF16), pltpu.VMEM((tm, d), F32)],
    )
    return pl.pallas_call(
        functools.partial(_expert_kernel, tm=tm, tf=tf, d=d),
        grid_spec=grid_spec,
        out_shape=jax.ShapeDtypeStruct((n_tiles * tm * pr, LANES), jnp.uint32),
        compiler_params=_cparams("arbitrary", "arbitrary"),
        name="expert_mlp",
    )(tile_expert, tile_valid, row_tok, row_tok, xp, w1p, b1g, b1l, w2, b2)


def _combine_kernel(dc_ref, dn_ref, x_ref, yp_ref, w_ref, lg_ref, lb_ref, o_ref, yg_ref, sem_ref, ysc_ref,
                    *, alpha, tt, d):
    i = pl.program_id(0)
    n_steps = pl.num_programs(0)
    pr = _packed_rows(d)
    slot = i % 2

    def start_gather(dest_ref, slot_):
        def body(t, carry):
            for k in range(TOP_K):
                src = yp_ref.at[pl.ds(pl.multiple_of(dest_ref[t * TOP_K + k] * pr, pr), pr)]
                dst = yg_ref.at[slot_, pl.ds(pl.multiple_of((k * tt + t) * pr, pr), pr)]
                pltpu.make_async_copy(src, dst, sem_ref.at[slot_]).start()
            return carry
        lax.fori_loop(0, tt, body, 0, unroll=2)

    @pl.when(i == 0)
    def _():
        start_gather(dc_ref, 0)

    pltpu.make_async_copy(yg_ref.at[slot], yg_ref.at[slot], sem_ref.at[slot]).wait()

    @pl.when(i + 1 < n_steps)
    def _():
        start_gather(dn_ref, 1 - slot)

    for c in range(pr):
        lo_cols = slice(c * LANES, (c + 1) * LANES)
        hi_cols = slice(d // 2 + c * LANES, d // 2 + (c + 1) * LANES)
        acc_lo = alpha * x_ref[:, lo_cols]
        acc_hi = alpha * x_ref[:, hi_cols]
        for k in range(TOP_K):
            lo, hi = _load_packed(yg_ref.at[slot], c, tt, pr, row0=k * tt)
            wk = w_ref[:, k:k + 1]
            acc_lo = acc_lo + wk * lo
            acc_hi = acc_hi + wk * hi
        ysc_ref[:, lo_cols] = acc_lo
        ysc_ref[:, hi_cols] = acc_hi
    o_ref[...] = _layer_norm(ysc_ref[...], lg_ref[...], lb_ref[...])


def _combine_ln(x, yp, dest, top_w, ln_g, ln_b, *, alpha, tt):
    n, d = x.shape
    pr = _packed_rows(d)
    n_steps = n // tt
    return pl.pallas_call(
        functools.partial(_combine_kernel, alpha=alpha, tt=tt, d=d),
        grid=(n_steps,),
        in_specs=[pl.BlockSpec((tt * TOP_K,), lambda i: (i,), memory_space=pltpu.SMEM),
                  pl.BlockSpec((tt * TOP_K,), lambda i: (jnp.minimum(i + 1, n_steps - 1),),
                               memory_space=pltpu.SMEM),
                  pl.BlockSpec((tt, d), lambda i: (i, 0)),
                  pl.BlockSpec(memory_space=pl.ANY),
                  pl.BlockSpec((tt, TOP_K), lambda i: (i, 0)),
                  _resident((1, d), lambda i: (0, 0)),
                  _resident((1, d), lambda i: (0, 0))],
        out_specs=pl.BlockSpec((tt, d), lambda i: (i, 0)),
        out_shape=jax.ShapeDtypeStruct((n, d), F32),
        scratch_shapes=[pltpu.VMEM((2, TOP_K * tt * pr, LANES), jnp.uint32), pltpu.SemaphoreType.DMA((2,)),
                        pltpu.VMEM((tt, d), F32)],
        compiler_params=_cparams("arbitrary"),
        name="combine_ln2",
    )(dest, dest, x, yp, top_w, ln_g, ln_b)


def _moe_layout(top_i, rank, counts, *, tm):
    n, k = top_i.shape
    n_exp = counts.shape[0]
    n_tiles = (n * k) // tm + n_exp
    padded = (counts + tm - 1) // tm * tm
    pends = jnp.cumsum(padded)
    pstarts = pends - padded
    first_row = jnp.sum(jnp.where(top_i[..., None] == jnp.arange(n_exp, dtype=I32), pstarts.astype(I32), 0), -1)
    dest = first_row + rank
    tile_start = jnp.arange(n_tiles, dtype=I32) * tm
    tile_expert = jnp.clip(jnp.searchsorted(pends, tile_start, side="right"), 0, n_exp - 1).astype(I32)
    tile_valid = jnp.clip(counts[tile_expert] - (tile_start - pstarts[tile_expert]), 0, tm).astype(I32)
    tok = jnp.broadcast_to(jnp.arange(n, dtype=I32)[:, None], (n, k))
    row_tok = jnp.zeros((n_tiles * tm,), I32).at[dest.reshape(-1)].set(tok.reshape(-1))
    return dest, row_tok, tile_expert, tile_valid


def _moe(x, xp, p, *, tm, tf, tt_router):
    top_i, top_w, rank, counts = _router(x, p["w_router"], p["b_router"], tt=tt_router)
    dest, row_tok, tile_expert, tile_valid = _moe_layout(top_i, rank, counts[0], tm=tm)
    yp = _expert_mlp(xp, row_tok, tile_expert, tile_valid, p["w1p"], p["b1g"], p["b1l"], p["w2"], p["b2"],
                     tm=tm, tf=tf)
    return yp, dest.reshape(-1), top_w


def _prepare_params(w_in, rpb_a, conv_w, conv_b, lru_wa, lru_ba, lru_wx, lru_bx, lru_lambda, lam_q1, lam_k1,
                    lam_q2, lam_k2, subln_g, w_proj_a, w_proj_b, w_proj_c, w_out, ln1_g, ln1_b, w_router,
                    b_router, w1, b1, w2, b2, ln2_g, ln2_b, *, d_model, kh_by_rows, diff_heads, w1_prep_tiles):
    depth = w_in.shape[0]
    gate0 = w_in.shape[2] - 3 * d_model
    layers = []
    for l in range(depth):
        lam_init = 0.8 - 0.6 * math.exp(-0.3 * l)
        lam = (jnp.exp(jnp.sum(lam_q1[l].astype(F32) * lam_k1[l].astype(F32)))
               - jnp.exp(jnp.sum(lam_q2[l].astype(F32) * lam_k2[l].astype(F32))) + lam_init)
        slopes = 2.0 ** (-8.0 * (jnp.arange(diff_heads, dtype=F32) + 1.0) / diff_heads)
        layers.append(dict(
            w_in=jnp.concatenate([w_in[l, :, gate0:], w_in[l, :, :gate0]], axis=1).astype(BF16),
            na_bias={kh: _na_bias_table(rpb_a[l], kh) for kh in kh_by_rows},
            conv_w=conv_w[l], conv_b=conv_b[l][None, :],
            lru_wa=lru_wa[l].astype(BF16), lru_ba=lru_ba[l], lru_wx=lru_wx[l].astype(BF16), lru_bx=lru_bx[l],
            lru_log_scale=-LRU_C * jax.nn.softplus(-lru_lambda[l].astype(F32)),
            diff_scalars=jnp.concatenate([jnp.stack([lam, jnp.asarray(1.0 - lam_init, F32)]), slopes]).astype(F32),
            subln_g=subln_g[l][None, :],
            w_proj_a=w_proj_a[l].astype(BF16), w_proj_b=w_proj_b[l].astype(BF16),
            w_proj_c=w_proj_c[l].astype(BF16), w_out=w_out[l].astype(BF16),
            ln1_g=ln1_g[l][None, :], ln1_b=ln1_b[l][None, :],
            w_router=w_router[l], b_router=b_router[l][None, :],
            w1p=_w1_prep(w1[l], td=w1_prep_tiles[0], tc=w1_prep_tiles[1]),
            b1g=b1[l, :, None, 0::2], b1l=b1[l, :, None, 1::2],
            w2=w2[l].astype(BF16), b2=b2[l][:, None, :],
            ln2_g=ln2_g[l][None, :], ln2_b=ln2_b[l][None, :],
        ))
    return layers


def _encoder_layer(x, p, groups, *, alpha, cfg):
    d = x.shape[1]
    proj = _in_proj(x, p["w_in"], cfg["proj_tm"], cfg["proj_tn"])
    na_w = cfg["na_heads"] * LANES
    rnn_w = p["lru_ba"].shape[1]
    dq = cfg["diff_heads"] * 2 * cfg["diff_dc"]
    c_qa = 3 * d
    c_xb = c_qa + 3 * na_w
    c_qc = c_xb + 2 * rnn_w
    oa, ob, oc = [], [], []
    for row0, batch, seq in groups:
        kh = min(NA_KH_MAX, seq // GRID_W)
        oa.append(_na_attention(proj, p["na_bias"][kh], row0=row0, batch=batch, seq=seq, heads=cfg["na_heads"],
                                col_q=c_qa, col_k=c_qa + na_w, col_v=c_qa + 2 * na_w))
        ob.append(_rglru(proj, p["conv_w"], p["conv_b"], p["lru_wa"], p["lru_ba"], p["lru_wx"], p["lru_bx"],
                         p["lru_log_scale"], row0=row0, batch=batch, seq=seq, col_x=c_xb, col_y=c_xb + rnn_w))
        oc.append(_diff_attention(proj, p["diff_scalars"], p["subln_g"], row0=row0, batch=batch, seq=seq,
                                  heads=cfg["diff_heads"], dc=cfg["diff_dc"], col_q=c_qc, col_k=c_qc + dq,
                                  col_v=c_qc + 2 * dq, tq=cfg["diff_tq"]))
    oa, ob, oc = (jnp.concatenate(t, axis=0) for t in (oa, ob, oc))
    x1, x1p = _merge_out_ln(x, proj, oa, ob, oc, p["w_proj_a"], p["w_proj_b"], p["w_proj_c"], p["w_out"],
                            p["ln1_g"], p["ln1_b"], alpha=alpha, tm=cfg["merge_tm"])
    yp, dest, top_w = _moe(x1, x1p, p, tm=cfg["moe_tm"], tf=cfg["moe_tf"], tt_router=cfg["router_tt"])
    return _combine_ln(x1, yp, dest, top_w, p["ln2_g"], p["ln2_b"], alpha=alpha, tt=cfg["combine_tt"])


DEFAULT_CFG = dict(proj_tm=1024, proj_tn=1024, na_heads=4, diff_heads=4, diff_dc=64, diff_tq=256, merge_tm=256,
                   moe_tm=512, moe_tf=512, router_tt=512, combine_tt=256, w1_prep_tiles=(512, 2048))


def _trunk(xs, params, cfg):
    d = xs[0].shape[-1]
    groups, row0 = [], 0
    for x in xs:
        groups.append((row0, x.shape[0], x.shape[1]))
        row0 += x.shape[0] * x.shape[1]
    kh_by_rows = sorted({min(NA_KH_MAX, s // GRID_W) for _, _, s in groups})
    layers = _prepare_params(*params, d_model=d, kh_by_rows=kh_by_rows, diff_heads=cfg["diff_heads"],
                             w1_prep_tiles=cfg["w1_prep_tiles"])
    alpha = (2 * len(layers)) ** 0.25
    x = jnp.concatenate([x.reshape(-1, d) for x in xs], axis=0)
    for p in layers:
        x = _encoder_layer(x, p, tuple(groups), alpha=alpha, cfg=cfg)
    outs = []
    for (r0, b, s), xin in zip(groups, xs):
        outs.append(x[r0:r0 + b * s].reshape(xin.shape))
    return tuple(outs)


def kernel(x_prompt, x_sample, w_in, rpb_a, conv_w, conv_b, lru_wa, lru_ba, lru_wx, lru_bx, lru_lambda, lam_q1,
           lam_k1, lam_q2, lam_k2, subln_g, w_proj_a, w_proj_b, w_proj_c, w_out, ln1_g, ln1_b, w_router,
           b_router, w1, b1, w2, b2, ln2_g, ln2_b):
    params = (w_in, rpb_a, conv_w, conv_b, lru_wa, lru_ba, lru_wx, lru_bx, lru_lambda, lam_q1, lam_k1, lam_q2,
              lam_k2, subln_g, w_proj_a, w_proj_b, w_proj_c, w_out, ln1_g, ln1_b, w_router, b_router, w1, b1,
              w2, b2, ln2_g, ln2_b)
    return _trunk((x_prompt, x_sample), params, DEFAULT_CFG)
```

```python
import functools
import math

import jax
import jax.numpy as jnp
from jax import lax
from jax.experimental import pallas as pl
from jax.experimental.pallas import tpu as pltpu

F32 = jnp.float32
BF16 = jnp.bfloat16
I32 = jnp.int32

GRID_W = 64
NA_KH_MAX = 8
NA_KW = 16
LRU_C = 8.0
TOP_K = 4
SWIGLU_LIMIT = 7.0
SWIGLU_ALPHA = 1.702
LN_EPS = 1e-5
NEG_INF = -1e30

LANES = 128
SUBLANES = 8
VMEM_LIMIT = 56 * 1024 * 1024


def _cparams(*sem):
    return pltpu.CompilerParams(dimension_semantics=sem, vmem_limit_bytes=VMEM_LIMIT)


def _resident(block_shape, index_map):
    return pl.BlockSpec(block_shape, index_map, pipeline_mode=pl.Buffered(1))


def _in_proj_kernel(x_ref, w_ref, o_ref, xb_ref):
    @pl.when(pl.program_id(1) == 0)
    def _():
        xb_ref[...] = x_ref[...].astype(BF16)

    o_ref[...] = jnp.dot(xb_ref[...], w_ref[...], preferred_element_type=F32).astype(o_ref.dtype)


def _in_proj(x, w, tm, tn):
    n, d = x.shape
    width = w.shape[1]
    return pl.pallas_call(
        _in_proj_kernel,
        grid=(n // tm, width // tn),
        in_specs=[pl.BlockSpec((tm, d), lambda i, j: (i, 0)),
                  pl.BlockSpec((d, tn), lambda i, j: (0, j))],
        out_specs=pl.BlockSpec((tm, tn), lambda i, j: (i, j)),
        out_shape=jax.ShapeDtypeStruct((n, width), BF16),
        scratch_shapes=[pltpu.VMEM((tm, d), BF16)],
        compiler_params=_cparams("parallel", "arbitrary"),
        name="in_proj",
    )(x, w)


def _na_bias_table(rpb, kh):
    c = jnp.arange(GRID_W)[:, None]
    kc = jnp.arange(GRID_W)[None, :]
    col_off = jnp.clip(kc - c + NA_KW - 1, 0, 2 * NA_KW - 2)
    win = jnp.clip(c - NA_KW // 2, 0, GRID_W - NA_KW)
    ok = (kc >= win) & (kc < win + NA_KW)
    onehot = (col_off[..., None] == jnp.arange(2 * NA_KW - 1)).astype(F32)
    cols = jnp.einsum("hrv,ckv->hrck", rpb.astype(F32), onehot, precision=lax.Precision.HIGHEST)
    cols = jnp.where(ok, cols, NEG_INF)
    tab = jnp.stack([cols[:, NA_KH_MAX - 1 - d:NA_KH_MAX - 1 - d + kh] for d in range(kh)], axis=1)
    return tab.transpose(0, 1, 3, 2, 4).reshape(rpb.shape[0], kh, GRID_W, kh * GRID_W)


def _na_kernel(q_ref, k_ref, v_ref, bias_ref, o_ref, *, rows, kh, scale):
    def row_body(i, carry):
        r0 = jnp.clip(i - kh // 2, 0, rows - kh)
        q = q_ref[pl.ds(pl.multiple_of(i * GRID_W, GRID_W), GRID_W), :]
        koff = pl.multiple_of(r0 * GRID_W, GRID_W)
        kw = k_ref[pl.ds(koff, kh * GRID_W), :]
        vw = v_ref[pl.ds(koff, kh * GRID_W), :]
        s = lax.dot_general(q, kw, (((1,), (1,)), ((), ())), preferred_element_type=F32)
        s = s * scale + bias_ref[i - r0]
        m = jnp.max(s, axis=-1, keepdims=True)
        p = jnp.exp(s - m)
        l = jnp.sum(p, axis=-1, keepdims=True)
        o = jnp.dot(p.astype(BF16), vw, preferred_element_type=F32) / l
        o_ref[pl.ds(pl.multiple_of(i * GRID_W, GRID_W), GRID_W), :] = o.astype(o_ref.dtype)
        return carry

    lax.fori_loop(0, rows, row_body, 0)


def _na_attention(proj, bias, *, row0, batch, seq, heads, col_q, col_k, col_v):
    hd = LANES
    rows = seq // GRID_W
    kh = bias.shape[1]
    rb = row0 // seq
    assert row0 % seq == 0 and rows >= kh
    spec = lambda col: pl.BlockSpec((seq, hd), lambda b, h: (rb + b, col // hd + h))
    return pl.pallas_call(
        functools.partial(_na_kernel, rows=rows, kh=kh, scale=hd ** -0.5),
        grid=(batch, heads),
        in_specs=[spec(col_q), spec(col_k), spec(col_v),
                  pl.BlockSpec((None, kh, GRID_W, kh * GRID_W), lambda b, h: (h, 0, 0, 0))],
        out_specs=pl.BlockSpec((seq, hd), lambda b, h: (b, h)),
        out_shape=jax.ShapeDtypeStruct((batch * seq, heads * hd), BF16),
        compiler_params=_cparams("parallel", "parallel"),
        name="na_attention",
    )(proj, proj, proj, bias)


def _lru_kernel(x_ref, y_ref, cw_ref, cb_ref, wa_ref, ba_ref, wx_ref, bx_ref, ls_ref, o_ref,
                a0_ref, b0_ref, a1_ref, b1_ref, *, seq):
    chunk = seq // SUBLANES
    x = x_ref[...].astype(F32)
    t_idx = lax.broadcasted_iota(I32, x.shape, 0)
    xc = (cw_ref[0:1, :] * jnp.where(t_idx >= 2, pltpu.roll(x, 2, 0), 0.0)
          + cw_ref[1:2, :] * jnp.where(t_idx >= 1, pltpu.roll(x, 1, 0), 0.0)
          + cw_ref[2:3, :] * x
          + cw_ref[3:4, :] * jnp.where(t_idx < seq - 1, pltpu.roll(x, seq - 1, 0), 0.0)
          + cb_ref[...])
    xb = xc.astype(BF16)
    for d, (a_ref, b_ref) in enumerate(((a0_ref, b0_ref), (a1_ref, b1_ref))):
        r = jax.nn.sigmoid(jnp.dot(xb, wa_ref[d], preferred_element_type=F32) + ba_ref[d:d + 1, :])
        ig = jax.nn.sigmoid(jnp.dot(xb, wx_ref[d], preferred_element_type=F32) + bx_ref[d:d + 1, :])
        a = jnp.exp(ls_ref[d:d + 1, :] * r)
        a_ref[...] = a
        b_ref[...] = jnp.sqrt(1.0 - a * a) * (ig * xc)

    def scan_body(t, carry):
        hf, pf, hb, pb = carry
        tb = chunk - 1 - t
        af = a0_ref[pl.ds(t, SUBLANES, stride=chunk), :]
        bf = b0_ref[pl.ds(t, SUBLANES, stride=chunk), :]
        ab = a1_ref[pl.ds(tb, SUBLANES, stride=chunk), :]
        bb = b1_ref[pl.ds(tb, SUBLANES, stride=chunk), :]
        hf = af * hf + bf
        pf = pf * af
        hb = ab * hb + bb
        pb = pb * ab
        b0_ref[pl.ds(t, SUBLANES, stride=chunk), :] = hf
        a0_ref[pl.ds(t, SUBLANES, stride=chunk), :] = pf
        b1_ref[pl.ds(tb, SUBLANES, stride=chunk), :] = hb
        a1_ref[pl.ds(tb, SUBLANES, stride=chunk), :] = pb
        return hf, pf, hb, pb

    zeros = jnp.zeros((SUBLANES, LANES), F32)
    ones = jnp.ones((SUBLANES, LANES), F32)
    hf, pf, hb, pb = lax.fori_loop(0, chunk, scan_body, (zeros, ones, zeros, ones), unroll=8)

    cf = [jnp.zeros((1, LANES), F32)]
    for j in range(SUBLANES - 1):
        cf.append(pf[j:j + 1] * cf[j] + hf[j:j + 1])
    cb = [jnp.zeros((1, LANES), F32)]
    for j in range(SUBLANES - 1, 0, -1):
        cb.append(pb[j:j + 1] * cb[-1] + hb[j:j + 1])
    cb = cb[::-1]
    for j in range(SUBLANES):
        sl = pl.ds(j * chunk, chunk)
        h = (b0_ref[sl, :] + a0_ref[sl, :] * cf[j]) + (b1_ref[sl, :] + a1_ref[sl, :] * cb[j])
        y = y_ref[sl, :].astype(F32)
        o_ref[sl, :] = (h * jax.nn.gelu(y, approximate=True)).astype(o_ref.dtype)


def _rglru(proj, conv_w, conv_b, wa, ba, wx, bx, log_scale, *, row0, batch, seq, col_x, col_y):
    nb = wa.shape[1]
    bw = wa.shape[2]
    assert bw == LANES and row0 % seq == 0 and seq % (SUBLANES * SUBLANES) == 0
    rb = row0 // seq
    cvec = lambda rows_: pl.BlockSpec((rows_, bw), lambda b, n: (0, n))
    wspec = pl.BlockSpec((2, None, bw, bw), lambda b, n: (0, n, 0, 0))
    return pl.pallas_call(
        functools.partial(_lru_kernel, seq=seq),
        grid=(batch, nb),
        in_specs=[pl.BlockSpec((seq, bw), lambda b, n: (rb + b, col_x // bw + n)),
                  pl.BlockSpec((seq, bw), lambda b, n: (rb + b, col_y // bw + n)),
                  cvec(conv_w.shape[0]), cvec(1), wspec, cvec(2), wspec, cvec(2), cvec(2)],
        out_specs=pl.BlockSpec((seq, bw), lambda b, n: (b, n)),
        out_shape=jax.ShapeDtypeStruct((batch * seq, nb * bw), BF16),
        scratch_shapes=[pltpu.VMEM((seq, bw), F32)] * 4,
        compiler_params=_cparams("parallel", "parallel"),
        name="rglru",
    )(proj, proj, conv_w, conv_b, wa, ba, wx, bx, log_scale)


def _diff_kernel(sc_ref, q_ref, k_ref, v_ref, g_ref, o_ref, *, tq, seq, dc, scale):
    h = pl.program_id(1)
    lam = sc_ref[0]
    out_scale = sc_ref[1]
    slope = sc_ref[2 + h]
    q = q_ref[...]
    k = k_ref[...]
    v = v_ref[...]
    lane = lax.broadcasted_iota(I32, q.shape, 1)
    qpos = pl.program_id(2) * tq + lax.broadcasted_iota(I32, (tq, 1), 0)
    kpos = lax.broadcasted_iota(I32, (1, seq), 1)
    bias = jnp.abs(qpos - kpos).astype(F32) * (-slope)
    outs = []
    for m_idx in range(2):
        in_map = (lane < dc) if m_idx == 0 else (lane >= dc)
        qm = jnp.where(in_map, q, jnp.zeros_like(q)) * jnp.asarray(scale, q.dtype)
        s = lax.dot_general(qm, k, (((1,), (1,)), ((), ())), preferred_element_type=F32) + bias
        mx = jnp.max(s, axis=-1, keepdims=True)
        p = jnp.exp(s - mx)
        l = jnp.sum(p, axis=-1, keepdims=True)
        outs.append(jnp.dot(p.astype(BF16), v, preferred_element_type=F32) / l)
    o = outs[0] - lam * outs[1]
    o = o * lax.rsqrt(jnp.mean(o * o, axis=-1, keepdims=True) + LN_EPS)
    o_ref[...] = (o * g_ref[...] * out_scale).astype(o_ref.dtype)


def _diff_attention(proj, scalars, subln_g, *, row0, batch, seq, heads, dc, col_q, col_k, col_v, tq):
    hd = 2 * dc
    assert hd == LANES and row0 % seq == 0 and seq % tq == 0
    assert math.log2(dc) % 2 == 0
    nq = seq // tq
    rb = row0 // seq
    return pl.pallas_call(
        functools.partial(_diff_kernel, tq=tq, seq=seq, dc=dc, scale=dc ** -0.5),
        grid=(batch, heads, nq),
        in_specs=[pl.BlockSpec(memory_space=pltpu.SMEM),
                  pl.BlockSpec((tq, hd), lambda b, h, i: ((rb + b) * nq + i, col_q // hd + h)),
                  pl.BlockSpec((seq, hd), lambda b, h, i: (rb + b, col_k // hd + h)),
                  pl.BlockSpec((seq, hd), lambda b, h, i: (rb + b, col_v // hd + h)),
                  pl.BlockSpec((1, hd), lambda b, h, i: (0, 0))],
        out_specs=pl.BlockSpec((tq, hd), lambda b, h, i: (b * nq + i, h)),
        out_shape=jax.ShapeDtypeStruct((batch * seq, heads * hd), BF16),
        compiler_params=_cparams("parallel", "parallel", "arbitrary"),
        name="diff_attention",
    )(scalars, proj, proj, proj, subln_g)


def _layer_norm(y, g, b):
    mu = jnp.mean(y, axis=-1, keepdims=True)
    yc = y - mu
    var = jnp.mean(yc * yc, axis=-1, keepdims=True)
    return yc * lax.rsqrt(var + LN_EPS) * g + b


def _packed_rows(d):
    return d // (2 * LANES)


def _store_packed(ref, y):
    m, d = y.shape
    pr = _packed_rows(d)
    lo = lax.bitcast_convert_type(y[:, :d // 2].astype(BF16).astype(F32), jnp.uint32)
    hi = lax.bitcast_convert_type(y[:, d // 2:].astype(BF16).astype(F32), jnp.uint32)
    word = hi | (lo >> 16)
    for c in range(pr):
        ref[pl.ds(c, m, stride=pr), :] = word[:, c * LANES:(c + 1) * LANES]


def _load_packed(ref, c, m, pr, row0=0):
    word = ref[pl.ds(row0 * pr + c, m, stride=pr), :]
    lo = lax.bitcast_convert_type(word << 16, F32)
    hi = lax.bitcast_convert_type(word & jnp.uint32(0xFFFF0000), F32)
    return lo, hi


def _merge_kernel(x_ref, g_ref, oa_ref, ob_ref, oc_ref, wa_ref, wb_ref, wc_ref, wo_ref, lg_ref, lb_ref,
                  o_ref, op_ref, *, d, alpha):
    merged = None
    for idx, (o_br, w_br) in enumerate(((oa_ref, wa_ref), (ob_ref, wb_ref), (oc_ref, wc_ref))):
        gate = jax.nn.sigmoid(g_ref[:, idx * d:(idx + 1) * d].astype(F32))
        term = gate * jnp.dot(o_br[...], w_br[...], preferred_element_type=F32)
        merged = term if merged is None else merged + term
    y = alpha * x_ref[...] + jnp.dot(merged.astype(BF16), wo_ref[...], preferred_element_type=F32)
    y = _layer_norm(y, lg_ref[...], lb_ref[...])
    o_ref[...] = y
    _store_packed(op_ref, y)


def _merge_out_ln(x, proj, oa, ob, oc, wa, wb, wc, wo, ln_g, ln_b, *, alpha, tm):
    n, d = x.shape
    pr = _packed_rows(d)
    row = lambda width: pl.BlockSpec((tm, width), lambda i: (i, 0))
    full = lambda arr: _resident(arr.shape, lambda i: (0, 0))
    return pl.pallas_call(
        functools.partial(_merge_kernel, d=d, alpha=alpha),
        grid=(n // tm,),
        in_specs=[row(d), row(3 * d), row(oa.shape[1]), row(ob.shape[1]), row(oc.shape[1]),
                  full(wa), full(wb), full(wc), full(wo), full(ln_g), full(ln_b)],
        out_specs=[row(d), pl.BlockSpec((tm * pr, LANES), lambda i: (i, 0))],
        out_shape=[jax.ShapeDtypeStruct((n, d), F32), jax.ShapeDtypeStruct((n * pr, LANES), jnp.uint32)],
        compiler_params=_cparams("parallel"),
        name="merge_out_ln1",
    )(x, proj, oa, ob, oc, wa, wb, wc, wo, ln_g, ln_b)


def _router_kernel(x_ref, w_ref, b_ref, idx_ref, wgt_ref, rank_ref, cnt_ref, carry_ref, *, tt, n_exp):
    @pl.when(pl.program_id(0) == 0)
    def _():
        carry_ref[...] = jnp.zeros_like(carry_ref)

    logits = jnp.dot(x_ref[...], w_ref[...], preferred_element_type=F32,
                     precision=lax.Precision.HIGHEST) + b_ref[...]
    lane = lax.broadcasted_iota(I32, (tt, n_exp), 1).astype(F32)
    work = logits
    vals, idxs, hots = [], [], []
    for _ in range(TOP_K):
        m = jnp.max(work, axis=-1, keepdims=True)
        idx = jnp.min(jnp.where(work == m, lane, float(n_exp)), axis=-1, keepdims=True)
        hot = lane == idx
        vals.append(m)
        idxs.append(idx)
        hots.append(hot)
        work = jnp.where(hot, -jnp.inf, work)
    exps = [jnp.exp(v - vals[0]) for v in vals]
    denom = exps[0]
    for e in exps[1:]:
        denom = denom + e
    onehot = jnp.zeros((tt, n_exp), F32)
    for hot in hots:
        onehot = onehot + hot.astype(F32)
    r_i = lax.broadcasted_iota(I32, (tt, tt), 0)
    c_i = lax.broadcasted_iota(I32, (tt, tt), 1)
    lower = jnp.where(r_i > c_i, 1.0, 0.0).astype(BF16)
    before = jnp.dot(lower, onehot.astype(BF16), preferred_element_type=F32) + carry_ref[...]
    for k in range(TOP_K):
        idx_ref[:, k:k + 1] = idxs[k].astype(I32)
        wgt_ref[:, k:k + 1] = exps[k] / denom
        rank_ref[:, k:k + 1] = jnp.sum(jnp.where(hots[k], before, 0.0), axis=-1, keepdims=True).astype(I32)
    carry_ref[...] = carry_ref[...] + jnp.sum(onehot, axis=0, keepdims=True)
    cnt_ref[...] = carry_ref[...].astype(I32)


def _router(x, w_router, b_router, *, tt):
    n, d = x.shape
    n_exp = w_router.shape[1]
    out4 = lambda dt: jax.ShapeDtypeStruct((n, TOP_K), dt)
    spec4 = pl.BlockSpec((tt, TOP_K), lambda i: (i, 0))
    return pl.pallas_call(
        functools.partial(_router_kernel, tt=tt, n_exp=n_exp),
        grid=(n // tt,),
        in_specs=[pl.BlockSpec((tt, d), lambda i: (i, 0)),
                  _resident((d, n_exp), lambda i: (0, 0)),
                  _resident((1, n_exp), lambda i: (0, 0))],
        out_specs=[spec4, spec4, spec4, pl.BlockSpec((1, n_exp), lambda i: (0, 0))],
        out_shape=[out4(I32), out4(F32), out4(I32), jax.ShapeDtypeStruct((1, n_exp), I32)],
        scratch_shapes=[pltpu.VMEM((1, n_exp), F32)],
        compiler_params=_cparams("arbitrary"),
        name="moe_router",
    )(x, w_router, b_router)


def _w1_prep_kernel(w_ref, o_ref):
    grp = 2 * LANES
    r = lax.broadcasted_iota(I32, (grp, grp), 0)
    c = lax.broadcasted_iota(I32, (grp, grp), 1)
    src = jnp.where(c < LANES, 2 * c, 2 * (c - LANES) + 1)
    perm = jnp.where(r == src, 1.0, 0.0).astype(BF16)
    for g in range(w_ref.shape[1] // grp):
        blk = w_ref[:, g * grp:(g + 1) * grp].astype(BF16)
        o_ref[:, g * grp:(g + 1) * grp] = jnp.dot(blk, perm, preferred_element_type=F32).astype(BF16)


def _w1_prep(w1, *, td, tc):
    n_exp, d, two_f = w1.shape
    spec = pl.BlockSpec((None, td, tc), lambda e, i, j: (e, i, j))
    return pl.pallas_call(
        _w1_prep_kernel,
        grid=(n_exp, d // td, two_f // tc),
        in_specs=[spec],
        out_specs=spec,
        out_shape=jax.ShapeDtypeStruct(w1.shape, BF16),
        compiler_params=_cparams("parallel", "parallel", "parallel"),
        name="w1_prep",
    )(w1)


def _expert_kernel(te_ref, tv_ref, rt_ref, rtn_ref, xp_ref, w1_ref, b1g_ref, b1l_ref, w2_ref, b2_ref, o_ref,
                   xg_ref, sem_ref, xb_ref, acc_ref, *, tm, tf, d, nf):
    i = pl.program_id(0)
    f = pl.program_id(1)
    last = nf - 1
    pr = _packed_rows(d)
    slot = i % 2
    live = tv_ref[i] > 0
    fetched = (i == 0) | (tv_ref[jnp.maximum(i - 1, 0)] > 0)

    def row_copy(tok_ref, r, slot_):
        src = xp_ref.at[pl.ds(pl.multiple_of(tok_ref[r] * pr, pr), pr)]
        dst = xg_ref.at[slot_, pl.ds(pl.multiple_of(r * pr, pr), pr)]
        return pltpu.make_async_copy(src, dst, sem_ref.at[slot_])

    @pl.when((i == 0) & (f == 0))
    def _():
        def body(r, carry):
            row_copy(rt_ref, r, 0).start()
            return carry
        lax.fori_loop(0, tm, body, 0, unroll=8)

    @pl.when(fetched & (f == 0))
    def _():
        pltpu.make_async_copy(xg_ref.at[slot], xg_ref.at[slot], sem_ref.at[slot]).wait()

    @pl.when(live & (f == 0))
    def _():
        for c in range(pr):
            lo, hi = _load_packed(xg_ref.at[slot], c, tm, pr)
            xb_ref[:, c * LANES:(c + 1) * LANES] = lo.astype(BF16)
            xb_ref[:, d // 2 + c * LANES:d // 2 + (c + 1) * LANES] = hi.astype(BF16)
        acc_ref[...] = jnp.zeros_like(acc_ref)

    @pl.when(live)
    def _():
        chunk = tm // nf
        for r in range(chunk):
            row_copy(rtn_ref, f * chunk + r, 1 - slot).start()
        h = jnp.dot(xb_ref[...], w1_ref[...], preferred_element_type=F32)
        acts = []
        for j in range(tf // LANES):
            hg = h[:, 2 * j * LANES:(2 * j + 1) * LANES] + b1g_ref[:, j * LANES:(j + 1) * LANES]
            hl = h[:, (2 * j + 1) * LANES:(2 * j + 2) * LANES] + b1l_ref[:, j * LANES:(j + 1) * LANES]
            glu = jnp.minimum(hg, SWIGLU_LIMIT)
            lin = jnp.clip(hl, -SWIGLU_LIMIT, SWIGLU_LIMIT)
            acts.append((glu * jax.nn.sigmoid(SWIGLU_ALPHA * glu) * (lin + 1.0)).astype(BF16))
        act = jnp.concatenate(acts, axis=1)
        acc_ref[...] += jnp.dot(act, w2_ref[...], preferred_element_type=F32)

    @pl.when(live & (f == last))
    def _():
        _store_packed(o_ref, acc_ref[...] + b2_ref[...])

    @pl.when(jnp.logical_not(live) & (f == last))
    def _():
        o_ref[...] = jnp.zeros_like(o_ref)


def _expert_mlp(xp, row_tok, tile_expert, tile_valid, w1p, b1g, b1l, w2, b2, *, tm, tf):
    d = w2.shape[2]
    pr = _packed_rows(d)
    n_tiles = row_tok.shape[0] // tm
    nf = w2.shape[1] // tf
    fidx = lambda f, tv, i: jnp.where(tv[i] > 0, f, nf - 1)
    grid_spec = pltpu.PrefetchScalarGridSpec(
        num_scalar_prefetch=2,
        grid=(n_tiles, nf),
        in_specs=[pl.BlockSpec((tm,), lambda i, f, te, tv: (i,), memory_space=pltpu.SMEM),
                  pl.BlockSpec((tm,), lambda i, f, te, tv: (jnp.minimum(i + 1, n_tiles - 1),),
                               memory_space=pltpu.SMEM),
                  pl.BlockSpec(memory_space=pl.ANY),
                  pl.BlockSpec((None, d, 2 * tf), lambda i, f, te, tv: (te[i], 0, fidx(f, tv, i))),
                  pl.BlockSpec((None, 1, tf), lambda i, f, te, tv: (te[i], 0, fidx(f, tv, i))),
                  pl.BlockSpec((None, 1, tf), lambda i, f, te, tv: (te[i], 0, fidx(f, tv, i))),
                  pl.BlockSpec((None, tf, d), lambda i, f, te, tv: (te[i], fidx(f, tv, i), 0)),
                  pl.BlockSpec((None, 1, d), lambda i, f, te, tv: (te[i], 0, 0))],
        out_specs=pl.BlockSpec((tm * pr, LANES), lambda i, f, te, tv: (i, 0)),
        scratch_shapes=[pltpu.VMEM((2, tm * pr, LANES), jnp.uint32), pltpu.SemaphoreType.DMA((2,)),
                        pltpu.VMEM((tm, d), BF16), pltpu.VMEM((tm, d), F32)],
    )
    return pl.pallas_call(
        functools.partial(_expert_kernel, tm=tm, tf=tf, d=d, nf=nf),
        grid_spec=grid_spec,
        out_shape=jax.ShapeDtypeStruct((n_tiles * tm * pr, LANES), jnp.uint32),
        compiler_params=_cparams("arbitrary", "arbitrary"),
        name="expert_mlp",
    )(tile_expert, tile_valid, row_tok, row_tok, xp, w1p, b1g, b1l, w2, b2)


def _combine_kernel(dc_ref, dn_ref, x_ref, yp_ref, w_ref, lg_ref, lb_ref, o_ref, yg_ref, sem_ref, ysc_ref,
                    *, alpha, tt, d):
    i = pl.program_id(0)
    n_steps = pl.num_programs(0)
    pr = _packed_rows(d)
    slot = i % 2

    def start_gather(dest_ref, slot_):
        def body(t, carry):
            for k in range(TOP_K):
                src = yp_ref.at[pl.ds(pl.multiple_of(dest_ref[t * TOP_K + k] * pr, pr), pr)]
                dst = yg_ref.at[slot_, pl.ds(pl.multiple_of((k * tt + t) * pr, pr), pr)]
                pltpu.make_async_copy(src, dst, sem_ref.at[slot_]).start()
            return carry
        lax.fori_loop(0, tt, body, 0, unroll=2)

    @pl.when(i == 0)
    def _():
        start_gather(dc_ref, 0)

    pltpu.make_async_copy(yg_ref.at[slot], yg_ref.at[slot], sem_ref.at[slot]).wait()

    @pl.when(i + 1 < n_steps)
    def _():
        start_gather(dn_ref, 1 - slot)

    for c in range(pr):
        lo_cols = slice(c * LANES, (c + 1) * LANES)
        hi_cols = slice(d // 2 + c * LANES, d // 2 + (c + 1) * LANES)
        acc_lo = alpha * x_ref[:, lo_cols]
        acc_hi = alpha * x_ref[:, hi_cols]
        for k in range(TOP_K):
            lo, hi = _load_packed(yg_ref.at[slot], c, tt, pr, row0=k * tt)
            wk = w_ref[:, k:k + 1]
            acc_lo = acc_lo + wk * lo
            acc_hi = acc_hi + wk * hi
        ysc_ref[:, lo_cols] = acc_lo
        ysc_ref[:, hi_cols] = acc_hi
    o_ref[...] = _layer_norm(ysc_ref[...], lg_ref[...], lb_ref[...])


def _combine_ln(x, yp, dest, top_w, ln_g, ln_b, *, alpha, tt, row0, nrows):
    d = x.shape[1]
    pr = _packed_rows(d)
    assert row0 % tt == 0 and nrows % tt == 0
    n_steps = nrows // tt
    b0 = row0 // tt
    return pl.pallas_call(
        functools.partial(_combine_kernel, alpha=alpha, tt=tt, d=d),
        grid=(n_steps,),
        in_specs=[pl.BlockSpec((tt * TOP_K,), lambda i: (b0 + i,), memory_space=pltpu.SMEM),
                  pl.BlockSpec((tt * TOP_K,), lambda i: (b0 + jnp.minimum(i + 1, n_steps - 1),),
                               memory_space=pltpu.SMEM),
                  pl.BlockSpec((tt, d), lambda i: (b0 + i, 0)),
                  pl.BlockSpec(memory_space=pl.ANY),
                  pl.BlockSpec((tt, TOP_K), lambda i: (b0 + i, 0)),
                  _resident((1, d), lambda i: (0, 0)),
                  _resident((1, d), lambda i: (0, 0))],
        out_specs=pl.BlockSpec((tt, d), lambda i: (i, 0)),
        out_shape=jax.ShapeDtypeStruct((nrows, d), F32),
        scratch_shapes=[pltpu.VMEM((2, TOP_K * tt * pr, LANES), jnp.uint32), pltpu.SemaphoreType.DMA((2,)),
                        pltpu.VMEM((tt, d), F32)],
        compiler_params=_cparams("arbitrary"),
        name="combine_ln2",
    )(dest, dest, x, yp, top_w, ln_g, ln_b)


def _moe_layout(top_i, rank, counts, *, tm):
    n, k = top_i.shape
    n_exp = counts.shape[0]
    n_tiles = (n * k) // tm + n_exp + 1
    padded = (counts + tm - 1) // tm * tm
    pends = jnp.cumsum(padded)
    pstarts = pends - padded
    first_row = jnp.sum(jnp.where(top_i[..., None] == jnp.arange(n_exp, dtype=I32), pstarts.astype(I32), 0), -1)
    dest = first_row + rank
    tile_start = jnp.arange(n_tiles, dtype=I32) * tm
    tile_expert = jnp.clip(jnp.searchsorted(pends, tile_start, side="right"), 0, n_exp - 1).astype(I32)
    tile_valid = jnp.clip(counts[tile_expert] - (tile_start - pstarts[tile_expert]), 0, tm).astype(I32)
    tok = jnp.broadcast_to(jnp.arange(n, dtype=I32)[:, None], (n, k))
    row_tok = jnp.zeros((n_tiles * tm,), I32).at[dest.reshape(-1)].set(tok.reshape(-1))
    return dest, row_tok, tile_expert, tile_valid


def _moe(x, xp, p, *, tm, tf, tt_router):
    top_i, top_w, rank, counts = _router(x, p["w_router"], p["b_router"], tt=tt_router)
    dest, row_tok, tile_expert, tile_valid = _moe_layout(top_i, rank, counts[0], tm=tm)
    yp = _expert_mlp(xp, row_tok, tile_expert, tile_valid, p["w1p"], p["b1g"], p["b1l"], p["w2"], p["b2"],
                     tm=tm, tf=tf)
    return yp, dest.reshape(-1), top_w


def _prepare_params(w_in, rpb_a, conv_w, conv_b, lru_wa, lru_ba, lru_wx, lru_bx, lru_lambda, lam_q1, lam_k1,
                    lam_q2, lam_k2, subln_g, w_proj_a, w_proj_b, w_proj_c, w_out, ln1_g, ln1_b, w_router,
                    b_router, w1, b1, w2, b2, ln2_g, ln2_b, *, d_model, kh_by_rows, diff_heads, w1_prep_tiles):
    depth = w_in.shape[0]
    gate0 = w_in.shape[2] - 3 * d_model
    layers = []
    for l in range(depth):
        lam_init = 0.8 - 0.6 * math.exp(-0.3 * l)
        lam = (jnp.exp(jnp.sum(lam_q1[l].astype(F32) * lam_k1[l].astype(F32)))
               - jnp.exp(jnp.sum(lam_q2[l].astype(F32) * lam_k2[l].astype(F32))) + lam_init)
        slopes = 2.0 ** (-8.0 * (jnp.arange(diff_heads, dtype=F32) + 1.0) / diff_heads)
        layers.append(dict(
            w_in=jnp.concatenate([w_in[l, :, gate0:], w_in[l, :, :gate0]], axis=1).astype(BF16),
            na_bias={kh: _na_bias_table(rpb_a[l], kh) for kh in kh_by_rows},
            conv_w=conv_w[l], conv_b=conv_b[l][None, :],
            lru_wa=lru_wa[l].astype(BF16), lru_ba=lru_ba[l], lru_wx=lru_wx[l].astype(BF16), lru_bx=lru_bx[l],
            lru_log_scale=-LRU_C * jax.nn.softplus(-lru_lambda[l].astype(F32)),
            diff_scalars=jnp.concatenate([jnp.stack([lam, jnp.asarray(1.0 - lam_init, F32)]), slopes]).astype(F32),
            subln_g=subln_g[l][None, :],
            w_proj_a=w_proj_a[l].astype(BF16), w_proj_b=w_proj_b[l].astype(BF16),
            w_proj_c=w_proj_c[l].astype(BF16), w_out=w_out[l].astype(BF16),
            ln1_g=ln1_g[l][None, :], ln1_b=ln1_b[l][None, :],
            w_router=w_router[l], b_router=b_router[l][None, :],
            w1p=_w1_prep(w1[l], td=w1_prep_tiles[0], tc=w1_prep_tiles[1]),
            b1g=b1[l, :, None, 0::2], b1l=b1[l, :, None, 1::2],
            w2=w2[l].astype(BF16), b2=b2[l][:, None, :],
            ln2_g=ln2_g[l][None, :], ln2_b=ln2_b[l][None, :],
        ))
    return layers


def _encoder_layer(x, p, groups, *, alpha, cfg, split_output):
    d = x.shape[1]
    proj = _in_proj(x, p["w_in"], cfg["proj_tm"], cfg["proj_tn"])
    na_w = cfg["na_heads"] * LANES
    rnn_w = p["lru_ba"].shape[1]
    dq = cfg["diff_heads"] * 2 * cfg["diff_dc"]
    c_qa = 3 * d
    c_xb = c_qa + 3 * na_w
    c_qc = c_xb + 2 * rnn_w
    oa, ob, oc = [], [], []
    for row0, batch, seq in groups:
        kh = min(NA_KH_MAX, seq // GRID_W)
        oa.append(_na_attention(proj, p["na_bias"][kh], row0=row0, batch=batch, seq=seq, heads=cfg["na_heads"],
                                col_q=c_qa, col_k=c_qa + na_w, col_v=c_qa + 2 * na_w))
        ob.append(_rglru(proj, p["conv_w"], p["conv_b"], p["lru_wa"], p["lru_ba"], p["lru_wx"], p["lru_bx"],
                         p["lru_log_scale"], row0=row0, batch=batch, seq=seq, col_x=c_xb, col_y=c_xb + rnn_w))
        oc.append(_diff_attention(proj, p["diff_scalars"], p["subln_g"], row0=row0, batch=batch, seq=seq,
                                  heads=cfg["diff_heads"], dc=cfg["diff_dc"], col_q=c_qc, col_k=c_qc + dq,
                                  col_v=c_qc + 2 * dq, tq=cfg["diff_tq"]))
    oa, ob, oc = (jnp.concatenate(t, axis=0) for t in (oa, ob, oc))
    x1, x1p = _merge_out_ln(x, proj, oa, ob, oc, p["w_proj_a"], p["w_proj_b"], p["w_proj_c"], p["w_out"],
                            p["ln1_g"], p["ln1_b"], alpha=alpha, tm=cfg["merge_tm"])
    yp, dest, top_w = _moe(x1, x1p, p, tm=cfg["moe_tm"], tf=cfg["moe_tf"], tt_router=cfg["router_tt"])
    combine = functools.partial(_combine_ln, x1, yp, dest, top_w, p["ln2_g"], p["ln2_b"], alpha=alpha,
                                tt=cfg["combine_tt"])
    if split_output:
        return tuple(combine(row0=row0, nrows=batch * seq) for row0, batch, seq in groups)
    return combine(row0=0, nrows=x.shape[0])


DEFAULT_CFG = dict(proj_tm=1024, proj_tn=1024, na_heads=4, diff_heads=4, diff_dc=64, diff_tq=256, merge_tm=256,
                   moe_tm=512, moe_tf=1024, router_tt=512, combine_tt=256, w1_prep_tiles=(512, 2048))


def _trunk(xs, params, cfg):
    d = xs[0].shape[-1]
    groups, row0 = [], 0
    for x in xs:
        groups.append((row0, x.shape[0], x.shape[1]))
        row0 += x.shape[0] * x.shape[1]
    kh_by_rows = sorted({min(NA_KH_MAX, s // GRID_W) for _, _, s in groups})
    layers = _prepare_params(*params, d_model=d, kh_by_rows=kh_by_rows, diff_heads=cfg["diff_heads"],
                             w1_prep_tiles=cfg["w1_prep_tiles"])
    alpha = (2 * len(layers)) ** 0.25
    x = jnp.concatenate([x.reshape(-1, d) for x in xs], axis=0)
    for l, p in enumerate(layers):
        x = _encoder_layer(x, p, tuple(groups), alpha=alpha, cfg=cfg, split_output=(l == len(layers) - 1))
    return tuple(o.reshape(xin.shape) for o, xin in zip(x, xs))


def kernel(x_prompt, x_sample, w_in, rpb_a, conv_w, conv_b, lru_wa, lru_ba, lru_wx, lru_bx, lru_lambda, lam_q1,
           lam_k1, lam_q2, lam_k2, subln_g, w_proj_a, w_proj_b, w_proj_c, w_out, ln1_g, ln1_b, w_router,
           b_router, w1, b1, w2, b2, ln2_g, ln2_b):
    params = (w_in, rpb_a, conv_w, conv_b, lru_wa, lru_ba, lru_wx, lru_bx, lru_lambda, lam_q1, lam_k1, lam_q2,
              lam_k2, subln_g, w_proj_a, w_proj_b, w_proj_c, w_out, ln1_g, ln1_b, w_router, b_router, w1, b1,
              w2, b2, ln2_g, ln2_b)
    return _trunk((x_prompt, x_sample), params, DEFAULT_CFG)
```

```python
import functools
import math

import jax
import jax.numpy as jnp
from jax import lax
from jax.experimental import pallas as pl
from jax.experimental.pallas import tpu as pltpu

F32 = jnp.float32
BF16 = jnp.bfloat16
I32 = jnp.int32

GRID_W = 64
NA_KH_MAX = 8
NA_KW = 16
LRU_C = 8.0
TOP_K = 4
SWIGLU_LIMIT = 7.0
SWIGLU_ALPHA = 1.702
LN_EPS = 1e-5
NEG_INF = -1e30

LANES = 128
SUBLANES = 8
VMEM_LIMIT = 56 * 1024 * 1024


def _cparams(*sem):
    return pltpu.CompilerParams(dimension_semantics=sem, vmem_limit_bytes=VMEM_LIMIT)


def _resident(block_shape, index_map):
    return pl.BlockSpec(block_shape, index_map, pipeline_mode=pl.Buffered(1))


def _stream_specs(xs, tm):
    offsets, specs, off = [], [], 0
    for x in xs:
        assert x.shape[0] % tm == 0
        nb = x.shape[0] // tm
        offsets.append(off)
        specs.append(pl.BlockSpec((tm, x.shape[1]),
                                  lambda *a, off=off, nb=nb: (jnp.clip(a[0] - off, 0, nb - 1), 0)))
        off += nb
    return tuple(offsets), specs, off


def _stream_tile(x_refs, offsets, i):
    x = x_refs[0][...]
    for ref, off in zip(x_refs[1:], offsets[1:]):
        x = jnp.where(i >= off, ref[...], x)
    return x


def _in_proj_kernel(*refs, offsets):
    x_refs, (w_ref, o_ref, xb_ref) = refs[:len(offsets)], refs[len(offsets):]

    @pl.when(pl.program_id(1) == 0)
    def _():
        xb_ref[...] = _stream_tile(x_refs, offsets, pl.program_id(0)).astype(BF16)

    o_ref[...] = jnp.dot(xb_ref[...], w_ref[...], preferred_element_type=F32).astype(o_ref.dtype)


def _in_proj(xs, w, tm, tn):
    d = xs[0].shape[1]
    width = w.shape[1]
    offsets, x_specs, n_blocks = _stream_specs(xs, tm)
    return pl.pallas_call(
        functools.partial(_in_proj_kernel, offsets=offsets),
        grid=(n_blocks, width // tn),
        in_specs=x_specs + [pl.BlockSpec((d, tn), lambda i, j: (0, j))],
        out_specs=pl.BlockSpec((tm, tn), lambda i, j: (i, j)),
        out_shape=jax.ShapeDtypeStruct((n_blocks * tm, width), BF16),
        scratch_shapes=[pltpu.VMEM((tm, d), BF16)],
        compiler_params=_cparams("parallel", "arbitrary"),
        name="in_proj",
    )(*xs, w)


def _na_bias_table(rpb, kh):
    c = jnp.arange(GRID_W)[:, None]
    kc = jnp.arange(GRID_W)[None, :]
    col_off = jnp.clip(kc - c + NA_KW - 1, 0, 2 * NA_KW - 2)
    win = jnp.clip(c - NA_KW // 2, 0, GRID_W - NA_KW)
    ok = (kc >= win) & (kc < win + NA_KW)
    onehot = (col_off[..., None] == jnp.arange(2 * NA_KW - 1)).astype(F32)
    cols = jnp.einsum("hrv,ckv->hrck", rpb.astype(F32), onehot, precision=lax.Precision.HIGHEST)
    cols = jnp.where(ok, cols, NEG_INF)
    tab = jnp.stack([cols[:, NA_KH_MAX - 1 - d:NA_KH_MAX - 1 - d + kh] for d in range(kh)], axis=1)
    return tab.transpose(0, 1, 3, 2, 4).reshape(rpb.shape[0], kh, GRID_W, kh * GRID_W)


def _na_kernel(q_ref, k_ref, v_ref, bias_ref, o_ref, *, rows, kh, scale):
    def row_body(i, carry):
        r0 = jnp.clip(i - kh // 2, 0, rows - kh)
        q = q_ref[pl.ds(pl.multiple_of(i * GRID_W, GRID_W), GRID_W), :]
        koff = pl.multiple_of(r0 * GRID_W, GRID_W)
        kw = k_ref[pl.ds(koff, kh * GRID_W), :]
        vw = v_ref[pl.ds(koff, kh * GRID_W), :]
        s = lax.dot_general(q, kw, (((1,), (1,)), ((), ())), preferred_element_type=F32)
        s = s * scale + bias_ref[i - r0]
        m = jnp.max(s, axis=-1, keepdims=True)
        p = jnp.exp(s - m)
        l = jnp.sum(p, axis=-1, keepdims=True)
        o = jnp.dot(p.astype(BF16), vw, preferred_element_type=F32) / l
        o_ref[pl.ds(pl.multiple_of(i * GRID_W, GRID_W), GRID_W), :] = o.astype(o_ref.dtype)
        return carry

    lax.fori_loop(0, rows, row_body, 0, unroll=4)


def _na_attention(proj, bias, *, row0, batch, seq, heads, col_q, col_k, col_v):
    hd = LANES
    rows = seq // GRID_W
    kh = bias.shape[1]
    rb = row0 // seq
    assert row0 % seq == 0 and rows >= kh
    spec = lambda col: pl.BlockSpec((seq, hd), lambda b, h: (rb + b, col // hd + h))
    return pl.pallas_call(
        functools.partial(_na_kernel, rows=rows, kh=kh, scale=hd ** -0.5),
        grid=(batch, heads),
        in_specs=[spec(col_q), spec(col_k), spec(col_v),
                  pl.BlockSpec((None, kh, GRID_W, kh * GRID_W), lambda b, h: (h, 0, 0, 0))],
        out_specs=pl.BlockSpec((seq, hd), lambda b, h: (b, h)),
        out_shape=jax.ShapeDtypeStruct((batch * seq, heads * hd), BF16),
        compiler_params=_cparams("parallel", "parallel"),
        name="na_attention",
    )(proj, proj, proj, bias)


def _lru_kernel(x_ref, y_ref, cw_ref, cb_ref, wa_ref, ba_ref, wx_ref, bx_ref, ls_ref, o_ref,
                a0_ref, b0_ref, a1_ref, b1_ref, *, seq):
    x = x_ref[...].astype(F32)
    t_idx = lax.broadcasted_iota(I32, x.shape, 0)
    xc = (cw_ref[0:1, :] * jnp.where(t_idx >= 2, pltpu.roll(x, 2, 0), 0.0)
          + cw_ref[1:2, :] * jnp.where(t_idx >= 1, pltpu.roll(x, 1, 0), 0.0)
          + cw_ref[2:3, :] * x
          + cw_ref[3:4, :] * jnp.where(t_idx < seq - 1, pltpu.roll(x, seq - 1, 0), 0.0)
          + cb_ref[...])
    xb = xc.astype(BF16)
    for d, (a_ref, b_ref) in enumerate(((a0_ref, b0_ref), (a1_ref, b1_ref))):
        r = jax.nn.sigmoid(jnp.dot(xb, wa_ref[d], preferred_element_type=F32) + ba_ref[d:d + 1, :])
        ig = jax.nn.sigmoid(jnp.dot(xb, wx_ref[d], preferred_element_type=F32) + bx_ref[d:d + 1, :])
        a = jnp.exp(ls_ref[d:d + 1, :] * r)
        a_ref[...] = a
        b_ref[...] = jnp.sqrt(1.0 - a * a) * (ig * xc)

    n_tiles = seq // SUBLANES
    row = lax.broadcasted_iota(I32, (SUBLANES, LANES), 0)

    def tile_scan(a, b, carry, reverse):
        for s in (1, 2, 4):
            if reverse:
                keep = row < SUBLANES - s
                shift = SUBLANES - s
            else:
                keep = row >= s
                shift = s
            a_sh = jnp.where(keep, pltpu.roll(a, shift, 0), 1.0)
            b_sh = jnp.where(keep, pltpu.roll(b, shift, 0), 0.0)
            b = a * b_sh + b
            a = a * a_sh
        e = 0 if reverse else SUBLANES - 1
        a_edge = jnp.broadcast_to(a[e:e + 1, :], (SUBLANES, LANES))
        b_edge = jnp.broadcast_to(b[e:e + 1, :], (SUBLANES, LANES))
        return b + a * carry, b_edge + a_edge * carry

    def scan_body(v, carry):
        cf, cb = carry
        rf = pl.multiple_of(v * SUBLANES, SUBLANES)
        rb = pl.multiple_of((n_tiles - 1 - v) * SUBLANES, SUBLANES)
        hf, cf = tile_scan(a0_ref[pl.ds(rf, SUBLANES), :], b0_ref[pl.ds(rf, SUBLANES), :], cf, False)
        hb, cb = tile_scan(a1_ref[pl.ds(rb, SUBLANES), :], b1_ref[pl.ds(rb, SUBLANES), :], cb, True)
        b0_ref[pl.ds(rf, SUBLANES), :] = hf
        b1_ref[pl.ds(rb, SUBLANES), :] = hb
        return cf, cb

    zeros = jnp.zeros((SUBLANES, LANES), F32)
    lax.fori_loop(0, n_tiles, scan_body, (zeros, zeros), unroll=4)

    h = b0_ref[...] + b1_ref[...]
    y = y_ref[...].astype(F32)
    o_ref[...] = (h * jax.nn.gelu(y, approximate=True)).astype(o_ref.dtype)


def _rglru(proj, conv_w, conv_b, wa, ba, wx, bx, log_scale, *, row0, batch, seq, col_x, col_y):
    nb = wa.shape[1]
    bw = wa.shape[2]
    assert bw == LANES and row0 % seq == 0 and seq % (4 * SUBLANES) == 0
    rb = row0 // seq
    cvec = lambda rows_: pl.BlockSpec((rows_, bw), lambda b, n: (0, n))
    wspec = pl.BlockSpec((2, None, bw, bw), lambda b, n: (0, n, 0, 0))
    return pl.pallas_call(
        functools.partial(_lru_kernel, seq=seq),
        grid=(batch, nb),
        in_specs=[pl.BlockSpec((seq, bw), lambda b, n: (rb + b, col_x // bw + n)),
                  pl.BlockSpec((seq, bw), lambda b, n: (rb + b, col_y // bw + n)),
                  cvec(conv_w.shape[0]), cvec(1), wspec, cvec(2), wspec, cvec(2), cvec(2)],
        out_specs=pl.BlockSpec((seq, bw), lambda b, n: (b, n)),
        out_shape=jax.ShapeDtypeStruct((batch * seq, nb * bw), BF16),
        scratch_shapes=[pltpu.VMEM((seq, bw), F32)] * 4,
        compiler_params=_cparams("parallel", "parallel"),
        name="rglru",
    )(proj, proj, conv_w, conv_b, wa, ba, wx, bx, log_scale)


def _diff_kernel(sc_ref, q_ref, k_ref, v_ref, g_ref, o_ref, *, tq, seq, dc, scale):
    h = pl.program_id(1)
    lam = sc_ref[0]
    out_scale = sc_ref[1]
    slope = sc_ref[2 + h]
    q = q_ref[...]
    k = k_ref[...]
    v = v_ref[...]
    lane = lax.broadcasted_iota(I32, q.shape, 1)
    qpos = pl.program_id(2) * tq + lax.broadcasted_iota(I32, (tq, 1), 0)
    kpos = lax.broadcasted_iota(I32, (1, seq), 1)
    bias = jnp.abs(qpos - kpos).astype(F32) * (-slope)
    outs = []
    for m_idx in range(2):
        in_map = (lane < dc) if m_idx == 0 else (lane >= dc)
        qm = jnp.where(in_map, q, jnp.zeros_like(q)) * jnp.asarray(scale, q.dtype)
        s = lax.dot_general(qm, k, (((1,), (1,)), ((), ())), preferred_element_type=F32) + bias
        mx = jnp.max(s, axis=-1, keepdims=True)
        p = jnp.exp(s - mx)
        l = jnp.sum(p, axis=-1, keepdims=True)
        outs.append(jnp.dot(p.astype(BF16), v, preferred_element_type=F32) / l)
    o = outs[0] - lam * outs[1]
    o = o * lax.rsqrt(jnp.mean(o * o, axis=-1, keepdims=True) + LN_EPS)
    o_ref[...] = (o * g_ref[...] * out_scale).astype(o_ref.dtype)


def _diff_attention(proj, scalars, subln_g, *, row0, batch, seq, heads, dc, col_q, col_k, col_v, tq):
    hd = 2 * dc
    assert hd == LANES and row0 % seq == 0 and seq % tq == 0
    assert math.log2(dc) % 2 == 0
    nq = seq // tq
    rb = row0 // seq
    return pl.pallas_call(
        functools.partial(_diff_kernel, tq=tq, seq=seq, dc=dc, scale=dc ** -0.5),
        grid=(batch, heads, nq),
        in_specs=[pl.BlockSpec(memory_space=pltpu.SMEM),
                  pl.BlockSpec((tq, hd), lambda b, h, i: ((rb + b) * nq + i, col_q // hd + h)),
                  pl.BlockSpec((seq, hd), lambda b, h, i: (rb + b, col_k // hd + h)),
                  pl.BlockSpec((seq, hd), lambda b, h, i: (rb + b, col_v // hd + h)),
                  pl.BlockSpec((1, hd), lambda b, h, i: (0, 0))],
        out_specs=pl.BlockSpec((tq, hd), lambda b, h, i: (b * nq + i, h)),
        out_shape=jax.ShapeDtypeStruct((batch * seq, heads * hd), BF16),
        compiler_params=_cparams("parallel", "parallel", "arbitrary"),
        name="diff_attention",
    )(scalars, proj, proj, proj, subln_g)


def _layer_norm(y, g, b):
    mu = jnp.mean(y, axis=-1, keepdims=True)
    yc = y - mu
    var = jnp.mean(yc * yc, axis=-1, keepdims=True)
    return yc * lax.rsqrt(var + LN_EPS) * g + b


def _packed_rows(d):
    return d // (2 * LANES)


def _store_packed(ref, y):
    m, d = y.shape
    pr = _packed_rows(d)
    lo = lax.bitcast_convert_type(y[:, :d // 2].astype(BF16).astype(F32), jnp.uint32)
    hi = lax.bitcast_convert_type(y[:, d // 2:].astype(BF16).astype(F32), jnp.uint32)
    word = hi | (lo >> 16)
    for c in range(pr):
        ref[pl.ds(c, m, stride=pr), :] = word[:, c * LANES:(c + 1) * LANES]


def _load_packed(ref, c, m, pr, row0=0):
    word = ref[pl.ds(row0 * pr + c, m, stride=pr), :]
    lo = lax.bitcast_convert_type(word << 16, F32)
    hi = lax.bitcast_convert_type(word & jnp.uint32(0xFFFF0000), F32)
    return lo, hi


def _merge_kernel(*refs, d, alpha, offsets):
    g = len(offsets)
    i = pl.program_id(0)
    x_refs, oa_refs, ob_refs, oc_refs = (refs[j * g:(j + 1) * g] for j in range(4))
    g_ref, wa_ref, wb_ref, wc_ref, wo_ref, lg_ref, lb_ref, o_ref, op_ref = refs[4 * g:]
    x = _stream_tile(x_refs, offsets, i)
    merged = None
    for idx, (o_refs, w_br) in enumerate(((oa_refs, wa_ref), (ob_refs, wb_ref), (oc_refs, wc_ref))):
        gate = jax.nn.sigmoid(g_ref[:, idx * d:(idx + 1) * d].astype(F32))
        term = gate * jnp.dot(_stream_tile(o_refs, offsets, i), w_br[...], preferred_element_type=F32)
        merged = term if merged is None else merged + term
    y = alpha * x + jnp.dot(merged.astype(BF16), wo_ref[...], preferred_element_type=F32)
    y = _layer_norm(y, lg_ref[...], lb_ref[...])
    o_ref[...] = y
    _store_packed(op_ref, y)


def _merge_out_ln(xs, proj, oas, obs, ocs, wa, wb, wc, wo, ln_g, ln_b, *, alpha, tm):
    d = xs[0].shape[1]
    pr = _packed_rows(d)
    offsets, x_specs, n_blocks = _stream_specs(xs, tm)
    branch_specs = []
    for os_ in (oas, obs, ocs):
        branch_specs += _stream_specs(os_, tm)[1]
    n = n_blocks * tm
    row = lambda width: pl.BlockSpec((tm, width), lambda i: (i, 0))
    full = lambda arr: _resident(arr.shape, lambda i: (0, 0))
    return pl.pallas_call(
        functools.partial(_merge_kernel, d=d, alpha=alpha, offsets=offsets),
        grid=(n_blocks,),
        in_specs=x_specs + branch_specs + [row(3 * d), full(wa), full(wb), full(wc), full(wo), full(ln_g),
                                           full(ln_b)],
        out_specs=[row(d), pl.BlockSpec((tm * pr, LANES), lambda i: (i, 0))],
        out_shape=[jax.ShapeDtypeStruct((n, d), F32), jax.ShapeDtypeStruct((n * pr, LANES), jnp.uint32)],
        compiler_params=_cparams("parallel"),
        name="merge_out_ln1",
    )(*xs, *oas, *obs, *ocs, proj, wa, wb, wc, wo, ln_g, ln_b)


def _router_kernel(x_ref, w_ref, b_ref, idx_ref, wgt_ref, rank_ref, cnt_ref, carry_ref, *, tt, n_exp):
    @pl.when(pl.program_id(0) == 0)
    def _():
        carry_ref[...] = jnp.zeros_like(carry_ref)

    logits = jnp.dot(x_ref[...], w_ref[...], preferred_element_type=F32,
                     precision=lax.Precision.HIGHEST) + b_ref[...]
    lane = lax.broadcasted_iota(I32, (tt, n_exp), 1).astype(F32)
    work = logits
    vals, idxs, hots = [], [], []
    for _ in range(TOP_K):
        m = jnp.max(work, axis=-1, keepdims=True)
        idx = jnp.min(jnp.where(work == m, lane, float(n_exp)), axis=-1, keepdims=True)
        hot = lane == idx
        vals.append(m)
        idxs.append(idx)
        hots.append(hot)
        work = jnp.where(hot, -jnp.inf, work)
    exps = [jnp.exp(v - vals[0]) for v in vals]
    denom = exps[0]
    for e in exps[1:]:
        denom = denom + e
    onehot = jnp.zeros((tt, n_exp), F32)
    for hot in hots:
        onehot = onehot + hot.astype(F32)
    r_i = lax.broadcasted_iota(I32, (tt, tt), 0)
    c_i = lax.broadcasted_iota(I32, (tt, tt), 1)
    lower = jnp.where(r_i > c_i, 1.0, 0.0).astype(BF16)
    before = jnp.dot(lower, onehot.astype(BF16), preferred_element_type=F32) + carry_ref[...]
    for k in range(TOP_K):
        idx_ref[:, k:k + 1] = idxs[k].astype(I32)
        wgt_ref[:, k:k + 1] = exps[k] / denom
        rank_ref[:, k:k + 1] = jnp.sum(jnp.where(hots[k], before, 0.0), axis=-1, keepdims=True).astype(I32)
    carry_ref[...] = carry_ref[...] + jnp.sum(onehot, axis=0, keepdims=True)
    cnt_ref[...] = carry_ref[...].astype(I32)


def _router(x, w_router, b_router, *, tt):
    n, d = x.shape
    n_exp = w_router.shape[1]
    out4 = lambda dt: jax.ShapeDtypeStruct((n, TOP_K), dt)
    spec4 = pl.BlockSpec((tt, TOP_K), lambda i: (i, 0))
    return pl.pallas_call(
        functools.partial(_router_kernel, tt=tt, n_exp=n_exp),
        grid=(n // tt,),
        in_specs=[pl.BlockSpec((tt, d), lambda i: (i, 0)),
                  _resident((d, n_exp), lambda i: (0, 0)),
                  _resident((1, n_exp), lambda i: (0, 0))],
        out_specs=[spec4, spec4, spec4, pl.BlockSpec((1, n_exp), lambda i: (0, 0))],
        out_shape=[out4(I32), out4(F32), out4(I32), jax.ShapeDtypeStruct((1, n_exp), I32)],
        scratch_shapes=[pltpu.VMEM((1, n_exp), F32)],
        compiler_params=_cparams("arbitrary"),
        name="moe_router",
    )(x, w_router, b_router)


def _w1_prep_kernel(w_ref, o_ref):
    grp = 2 * LANES
    r = lax.broadcasted_iota(I32, (grp, grp), 0)
    c = lax.broadcasted_iota(I32, (grp, grp), 1)
    src = jnp.where(c < LANES, 2 * c, 2 * (c - LANES) + 1)
    perm = jnp.where(r == src, 1.0, 0.0).astype(BF16)
    for g in range(w_ref.shape[1] // grp):
        blk = w_ref[:, g * grp:(g + 1) * grp].astype(BF16)
        o_ref[:, g * grp:(g + 1) * grp] = jnp.dot(blk, perm, preferred_element_type=F32).astype(BF16)


def _w1_prep(w1, *, td, tc):
    n_exp, d, two_f = w1.shape
    spec = pl.BlockSpec((None, td, tc), lambda e, i, j: (e, i, j))
    return pl.pallas_call(
        _w1_prep_kernel,
        grid=(n_exp, d // td, two_f // tc),
        in_specs=[spec],
        out_specs=spec,
        out_shape=jax.ShapeDtypeStruct(w1.shape, BF16),
        compiler_params=_cparams("parallel", "parallel", "parallel"),
        name="w1_prep",
    )(w1)


def _expert_kernel(te_ref, tv_ref, rt_ref, rtn_ref, xp_ref, w1_ref, b1g_ref, b1l_ref, w2_ref, b2_ref, o_ref,
                   xg_ref, sem_ref, xb_ref, acc_ref, *, tm, tf, d, nf):
    i = pl.program_id(0)
    f = pl.program_id(1)
    last = nf - 1
    pr = _packed_rows(d)
    slot = i % 2
    live = tv_ref[i] > 0
    fetched = (i == 0) | (tv_ref[jnp.maximum(i - 1, 0)] > 0)

    def row_copy(tok_ref, r, slot_):
        src = xp_ref.at[pl.ds(pl.multiple_of(tok_ref[r] * pr, pr), pr)]
        dst = xg_ref.at[slot_, pl.ds(pl.multiple_of(r * pr, pr), pr)]
        return pltpu.make_async_copy(src, dst, sem_ref.at[slot_])

    @pl.when((i == 0) & (f == 0))
    def _():
        def body(r, carry):
            row_copy(rt_ref, r, 0).start()
            return carry
        lax.fori_loop(0, tm, body, 0, unroll=8)

    @pl.when(fetched & (f == 0))
    def _():
        pltpu.make_async_copy(xg_ref.at[slot], xg_ref.at[slot], sem_ref.at[slot]).wait()

    @pl.when(live & (f == 0))
    def _():
        for c in range(pr):
            lo, hi = _load_packed(xg_ref.at[slot], c, tm, pr)
            xb_ref[:, c * LANES:(c + 1) * LANES] = lo.astype(BF16)
            xb_ref[:, d // 2 + c * LANES:d // 2 + (c + 1) * LANES] = hi.astype(BF16)
        acc_ref[...] = jnp.zeros_like(acc_ref)

    @pl.when(live)
    def _():
        chunk = tm // nf
        for r in range(chunk):
            row_copy(rtn_ref, f * chunk + r, 1 - slot).start()
        h = jnp.dot(xb_ref[...], w1_ref[...], preferred_element_type=F32)
        acts = []
        for j in range(tf // LANES):
            hg = h[:, 2 * j * LANES:(2 * j + 1) * LANES] + b1g_ref[:, j * LANES:(j + 1) * LANES]
            hl = h[:, (2 * j + 1) * LANES:(2 * j + 2) * LANES] + b1l_ref[:, j * LANES:(j + 1) * LANES]
            glu = jnp.minimum(hg, SWIGLU_LIMIT)
            lin = jnp.clip(hl, -SWIGLU_LIMIT, SWIGLU_LIMIT)
            acts.append((glu * jax.nn.sigmoid(SWIGLU_ALPHA * glu) * (lin + 1.0)).astype(BF16))
        act = jnp.concatenate(acts, axis=1)
        acc_ref[...] += jnp.dot(act, w2_ref[...], preferred_element_type=F32)

    @pl.when(live & (f == last))
    def _():
        _store_packed(o_ref, acc_ref[...] + b2_ref[...])

    @pl.when(jnp.logical_not(live) & (f == last))
    def _():
        o_ref[...] = jnp.zeros_like(o_ref)


def _expert_mlp(xp, row_tok, tile_expert, tile_valid, w1p, b1g, b1l, w2, b2, *, tm, tf):
    d = w2.shape[2]
    pr = _packed_rows(d)
    n_tiles = row_tok.shape[0] // tm
    nf = w2.shape[1] // tf
    fidx = lambda f, tv, i: jnp.where(tv[i] > 0, f, nf - 1)
    grid_spec = pltpu.PrefetchScalarGridSpec(
        num_scalar_prefetch=2,
        grid=(n_tiles, nf),
        in_specs=[pl.BlockSpec((tm,), lambda i, f, te, tv: (i,), memory_space=pltpu.SMEM),
                  pl.BlockSpec((tm,), lambda i, f, te, tv: (jnp.minimum(i + 1, n_tiles - 1),),
                               memory_space=pltpu.SMEM),
                  pl.BlockSpec(memory_space=pl.ANY),
                  pl.BlockSpec((None, d, 2 * tf), lambda i, f, te, tv: (te[i], 0, fidx(f, tv, i))),
                  pl.BlockSpec((None, 1, tf), lambda i, f, te, tv: (te[i], 0, fidx(f, tv, i))),
                  pl.BlockSpec((None, 1, tf), lambda i, f, te, tv: (te[i], 0, fidx(f, tv, i))),
                  pl.BlockSpec((None, tf, d), lambda i, f, te, tv: (te[i], fidx(f, tv, i), 0)),
                  pl.BlockSpec((None, 1, d), lambda i, f, te, tv: (te[i], 0, 0))],
        out_specs=pl.BlockSpec((tm * pr, LANES), lambda i, f, te, tv: (i, 0)),
        scratch_shapes=[pltpu.VMEM((2, tm * pr, LANES), jnp.uint32), pltpu.SemaphoreType.DMA((2,)),
                        pltpu.VMEM((tm, d), BF16), pltpu.VMEM((tm, d), F32)],
    )
    return pl.pallas_call(
        functools.partial(_expert_kernel, tm=tm, tf=tf, d=d, nf=nf),
        grid_spec=grid_spec,
        out_shape=jax.ShapeDtypeStruct((n_tiles * tm * pr, LANES), jnp.uint32),
        compiler_params=_cparams("arbitrary", "arbitrary"),
        name="expert_mlp",
    )(tile_expert, tile_valid, row_tok, row_tok, xp, w1p, b1g, b1l, w2, b2)


def _combine_kernel(dc_ref, dn_ref, x_ref, yp_ref, w_ref, lg_ref, lb_ref, o_ref, yg_ref, sem_ref, ysc_ref,
                    *, alpha, tt, d):
    i = pl.program_id(0)
    n_steps = pl.num_programs(0)
    pr = _packed_rows(d)
    slot = i % 2

    def start_rows(dest_ref, t, slot_):
        for k in range(TOP_K):
            src = yp_ref.at[pl.ds(pl.multiple_of(dest_ref[t * TOP_K + k] * pr, pr), pr)]
            dst = yg_ref.at[slot_, pl.ds(pl.multiple_of((k * tt + t) * pr, pr), pr)]
            pltpu.make_async_copy(src, dst, sem_ref.at[slot_]).start()

    def wait_slot(slot_):
        pltpu.make_async_copy(yg_ref.at[slot_], yg_ref.at[slot_], sem_ref.at[slot_]).wait()

    @pl.when(i == 0)
    def _():
        def body(t, carry):
            start_rows(dc_ref, t, 0)
            return carry
        lax.fori_loop(0, tt, body, 0, unroll=2)

    wait_slot(slot)

    tokens_per_group = tt // pr
    for c in range(pr):
        for t in range(c * tokens_per_group, (c + 1) * tokens_per_group):
            start_rows(dn_ref, t, 1 - slot)
        lo_cols = slice(c * LANES, (c + 1) * LANES)
        hi_cols = slice(d // 2 + c * LANES, d // 2 + (c + 1) * LANES)
        acc_lo = alpha * x_ref[:, lo_cols]
        acc_hi = alpha * x_ref[:, hi_cols]
        for k in range(TOP_K):
            lo, hi = _load_packed(yg_ref.at[slot], c, tt, pr, row0=k * tt)
            wk = w_ref[:, k:k + 1]
            acc_lo = acc_lo + wk * lo
            acc_hi = acc_hi + wk * hi
        ysc_ref[:, lo_cols] = acc_lo
        ysc_ref[:, hi_cols] = acc_hi
    o_ref[...] = _layer_norm(ysc_ref[...], lg_ref[...], lb_ref[...])

    @pl.when(i == n_steps - 1)
    def _():
        wait_slot(1 - slot)


def _combine_ln(x, yp, dest, top_w, ln_g, ln_b, *, alpha, tt, row0, nrows):
    d = x.shape[1]
    pr = _packed_rows(d)
    assert row0 % tt == 0 and nrows % tt == 0
    n_steps = nrows // tt
    b0 = row0 // tt
    return pl.pallas_call(
        functools.partial(_combine_kernel, alpha=alpha, tt=tt, d=d),
        grid=(n_steps,),
        in_specs=[pl.BlockSpec((tt * TOP_K,), lambda i: (b0 + i,), memory_space=pltpu.SMEM),
                  pl.BlockSpec((tt * TOP_K,), lambda i: (b0 + jnp.minimum(i + 1, n_steps - 1),),
                               memory_space=pltpu.SMEM),
                  pl.BlockSpec((tt, d), lambda i: (b0 + i, 0)),
                  pl.BlockSpec(memory_space=pl.ANY),
                  pl.BlockSpec((tt, TOP_K), lambda i: (b0 + i, 0)),
                  _resident((1, d), lambda i: (0, 0)),
                  _resident((1, d), lambda i: (0, 0))],
        out_specs=pl.BlockSpec((tt, d), lambda i: (i, 0)),
        out_shape=jax.ShapeDtypeStruct((nrows, d), F32),
        scratch_shapes=[pltpu.VMEM((2, TOP_K * tt * pr, LANES), jnp.uint32), pltpu.SemaphoreType.DMA((2,)),
                        pltpu.VMEM((tt, d), F32)],
        compiler_params=_cparams("arbitrary"),
        name="combine_ln2",
    )(dest, dest, x, yp, top_w, ln_g, ln_b)


def _moe_layout(top_i, rank, counts, *, tm):
    n, k = top_i.shape
    n_exp = counts.shape[0]
    n_tiles = (n * k) // tm + n_exp + 1
    padded = (counts + tm - 1) // tm * tm
    pends = jnp.cumsum(padded)
    pstarts = pends - padded
    first_row = jnp.sum(jnp.where(top_i[..., None] == jnp.arange(n_exp, dtype=I32), pstarts.astype(I32), 0), -1)
    dest = first_row + rank
    tile_start = jnp.arange(n_tiles, dtype=I32) * tm
    tile_expert = jnp.clip(jnp.searchsorted(pends, tile_start, side="right"), 0, n_exp - 1).astype(I32)
    tile_valid = jnp.clip(counts[tile_expert] - (tile_start - pstarts[tile_expert]), 0, tm).astype(I32)
    tok = jnp.broadcast_to(jnp.arange(n, dtype=I32)[:, None], (n, k))
    row_tok = jnp.zeros((n_tiles * tm,), I32).at[dest.reshape(-1)].set(tok.reshape(-1))
    return dest, row_tok, tile_expert, tile_valid


def _moe(x, xp, p, *, tm, tf, tt_router):
    top_i, top_w, rank, counts = _router(x, p["w_router"], p["b_router"], tt=tt_router)
    dest, row_tok, tile_expert, tile_valid = _moe_layout(top_i, rank, counts[0], tm=tm)
    yp = _expert_mlp(xp, row_tok, tile_expert, tile_valid, p["w1p"], p["b1g"], p["b1l"], p["w2"], p["b2"],
                     tm=tm, tf=tf)
    return yp, dest.reshape(-1), top_w


def _prepare_params(w_in, rpb_a, conv_w, conv_b, lru_wa, lru_ba, lru_wx, lru_bx, lru_lambda, lam_q1, lam_k1,
                    lam_q2, lam_k2, subln_g, w_proj_a, w_proj_b, w_proj_c, w_out, ln1_g, ln1_b, w_router,
                    b_router, w1, b1, w2, b2, ln2_g, ln2_b, *, d_model, kh_by_rows, diff_heads, w1_prep_tiles):
    depth = w_in.shape[0]
    gate0 = w_in.shape[2] - 3 * d_model
    layers = []
    for l in range(depth):
        lam_init = 0.8 - 0.6 * math.exp(-0.3 * l)
        lam = (jnp.exp(jnp.sum(lam_q1[l].astype(F32) * lam_k1[l].astype(F32)))
               - jnp.exp(jnp.sum(lam_q2[l].astype(F32) * lam_k2[l].astype(F32))) + lam_init)
        slopes = 2.0 ** (-8.0 * (jnp.arange(diff_heads, dtype=F32) + 1.0) / diff_heads)
        layers.append(dict(
            w_in=jnp.concatenate([w_in[l, :, gate0:], w_in[l, :, :gate0]], axis=1).astype(BF16),
            na_bias={kh: _na_bias_table(rpb_a[l], kh) for kh in kh_by_rows},
            conv_w=conv_w[l], conv_b=conv_b[l][None, :],
            lru_wa=lru_wa[l].astype(BF16), lru_ba=lru_ba[l], lru_wx=lru_wx[l].astype(BF16), lru_bx=lru_bx[l],
            lru_log_scale=-LRU_C * jax.nn.softplus(-lru_lambda[l].astype(F32)),
            diff_scalars=jnp.concatenate([jnp.stack([lam, jnp.asarray(1.0 - lam_init, F32)]), slopes]).astype(F32),
            subln_g=subln_g[l][None, :],
            w_proj_a=w_proj_a[l].astype(BF16), w_proj_b=w_proj_b[l].astype(BF16),
            w_proj_c=w_proj_c[l].astype(BF16), w_out=w_out[l].astype(BF16),
            ln1_g=ln1_g[l][None, :], ln1_b=ln1_b[l][None, :],
            w_router=w_router[l], b_router=b_router[l][None, :],
            w1p=_w1_prep(w1[l], td=w1_prep_tiles[0], tc=w1_prep_tiles[1]),
            b1g=b1[l, :, None, 0::2], b1l=b1[l, :, None, 1::2],
            w2=w2[l].astype(BF16), b2=b2[l][:, None, :],
            ln2_g=ln2_g[l][None, :], ln2_b=ln2_b[l][None, :],
        ))
    return layers


def _encoder_layer(xs, p, groups, *, alpha, cfg):
    d = xs[0].shape[1]
    proj = _in_proj(xs, p["w_in"], cfg["proj_tm"], cfg["proj_tn"])
    na_w = cfg["na_heads"] * LANES
    rnn_w = p["lru_ba"].shape[1]
    dq = cfg["diff_heads"] * 2 * cfg["diff_dc"]
    c_qa = 3 * d
    c_xb = c_qa + 3 * na_w
    c_qc = c_xb + 2 * rnn_w
    oa, ob, oc = [], [], []
    for row0, batch, seq in groups:
        kh = min(NA_KH_MAX, seq // GRID_W)
        oa.append(_na_attention(proj, p["na_bias"][kh], row0=row0, batch=batch, seq=seq, heads=cfg["na_heads"],
                                col_q=c_qa, col_k=c_qa + na_w, col_v=c_qa + 2 * na_w))
        ob.append(_rglru(proj, p["conv_w"], p["conv_b"], p["lru_wa"], p["lru_ba"], p["lru_wx"], p["lru_bx"],
                         p["lru_log_scale"], row0=row0, batch=batch, seq=seq, col_x=c_xb, col_y=c_xb + rnn_w))
        oc.append(_diff_attention(proj, p["diff_scalars"], p["subln_g"], row0=row0, batch=batch, seq=seq,
                                  heads=cfg["diff_heads"], dc=cfg["diff_dc"], col_q=c_qc, col_k=c_qc + dq,
                                  col_v=c_qc + 2 * dq, tq=cfg["diff_tq"]))
    x1, x1p = _merge_out_ln(xs, proj, oa, ob, oc, p["w_proj_a"], p["w_proj_b"], p["w_proj_c"], p["w_out"],
                            p["ln1_g"], p["ln1_b"], alpha=alpha, tm=cfg["merge_tm"])
    yp, dest, top_w = _moe(x1, x1p, p, tm=cfg["moe_tm"], tf=cfg["moe_tf"], tt_router=cfg["router_tt"])
    return tuple(_combine_ln(x1, yp, dest, top_w, p["ln2_g"], p["ln2_b"], alpha=alpha, tt=cfg["combine_tt"],
                             row0=row0, nrows=batch * seq) for row0, batch, seq in groups)


DEFAULT_CFG = dict(proj_tm=1024, proj_tn=1024, na_heads=4, diff_heads=4, diff_dc=64, diff_tq=256, merge_tm=256,
                   moe_tm=512, moe_tf=1024, router_tt=512, combine_tt=256, w1_prep_tiles=(512, 2048))


def _trunk(xs, params, cfg):
    d = xs[0].shape[-1]
    groups, row0 = [], 0
    for x in xs:
        groups.append((row0, x.shape[0], x.shape[1]))
        row0 += x.shape[0] * x.shape[1]
    kh_by_rows = sorted({min(NA_KH_MAX, s // GRID_W) for _, _, s in groups})
    layers = _prepare_params(*params, d_model=d, kh_by_rows=kh_by_rows, diff_heads=cfg["diff_heads"],
                             w1_prep_tiles=cfg["w1_prep_tiles"])
    alpha = (2 * len(layers)) ** 0.25
    streams = tuple(x.reshape(-1, d) for x in xs)
    for p in layers:
        streams = _encoder_layer(streams, p, tuple(groups), alpha=alpha, cfg=cfg)
    return tuple(o.reshape(xin.shape) for o, xin in zip(streams, xs))


def kernel(x_prompt, x_sample, w_in, rpb_a, conv_w, conv_b, lru_wa, lru_ba, lru_wx, lru_bx, lru_lambda, lam_q1,
           lam_k1, lam_q2, lam_k2, subln_g, w_proj_a, w_proj_b, w_proj_c, w_out, ln1_g, ln1_b, w_router,
           b_router, w1, b1, w2, b2, ln2_g, ln2_b):
    params = (w_in, rpb_a, conv_w, conv_b, lru_wa, lru_ba, lru_wx, lru_bx, lru_lambda, lam_q1, lam_k1, lam_q2,
              lam_k2, subln_g, w_proj_a, w_proj_b, w_proj_c, w_out, ln1_g, ln1_b, w_router, b_router, w1, b1,
              w2, b2, ln2_g, ln2_b)
    return _trunk((x_prompt, x_sample), params, DEFAULT_CFG)
```

```python
import functools
import math

import jax
import jax.numpy as jnp
from jax import lax
from jax.experimental import pallas as pl
from jax.experimental.pallas import tpu as pltpu

F32 = jnp.float32
BF16 = jnp.bfloat16
I32 = jnp.int32

GRID_W = 64
NA_KH_MAX = 8
NA_KW = 16
NA_GROUP = 8
EXPERT_LOOKAHEAD = 2
LRU_C = 8.0
TOP_K = 4
SWIGLU_LIMIT = 7.0
SWIGLU_ALPHA = 1.702
LN_EPS = 1e-5
NEG_INF = -1e30

LANES = 128
SUBLANES = 8
VMEM_LIMIT = 56 * 1024 * 1024


def _cparams(*sem):
    return pltpu.CompilerParams(dimension_semantics=sem, vmem_limit_bytes=VMEM_LIMIT)


def _resident(block_shape, index_map):
    return pl.BlockSpec(block_shape, index_map, pipeline_mode=pl.Buffered(1))


def _stream_specs(xs, tm):
    offsets, specs, off = [], [], 0
    for x in xs:
        assert x.shape[0] % tm == 0
        nb = x.shape[0] // tm
        offsets.append(off)
        specs.append(pl.BlockSpec((tm, x.shape[1]),
                                  lambda *a, off=off, nb=nb: (jnp.clip(a[0] - off, 0, nb - 1), 0)))
        off += nb
    return tuple(offsets), specs, off


def _stream_tile(x_refs, offsets, i):
    x = x_refs[0][...]
    for ref, off in zip(x_refs[1:], offsets[1:]):
        x = jnp.where(i >= off, ref[...], x)
    return x


def _in_proj_kernel(*refs, offsets):
    x_refs, (w_ref, o_ref, xb_ref) = refs[:len(offsets)], refs[len(offsets):]

    @pl.when(pl.program_id(1) == 0)
    def _():
        xb_ref[...] = _stream_tile(x_refs, offsets, pl.program_id(0)).astype(BF16)

    o_ref[...] = jnp.dot(xb_ref[...], w_ref[...], preferred_element_type=F32).astype(o_ref.dtype)


def _in_proj(xs, w, tm, tn):
    d = xs[0].shape[1]
    width = w.shape[1]
    offsets, x_specs, n_blocks = _stream_specs(xs, tm)
    return pl.pallas_call(
        functools.partial(_in_proj_kernel, offsets=offsets),
        grid=(n_blocks, width // tn),
        in_specs=x_specs + [pl.BlockSpec((d, tn), lambda i, j: (0, j))],
        out_specs=pl.BlockSpec((tm, tn), lambda i, j: (i, j)),
        out_shape=jax.ShapeDtypeStruct((n_blocks * tm, width), BF16),
        scratch_shapes=[pltpu.VMEM((tm, d), BF16)],
        compiler_params=_cparams("parallel", "arbitrary"),
        name="in_proj",
    )(*xs, w)


def _na_bias_table(rpb, rows):
    kh, grp = NA_KH_MAX, NA_GROUP
    span = min(2 * grp, rows)
    c = jnp.arange(GRID_W)[:, None]
    kc = jnp.arange(GRID_W)[None, :]
    col_off = jnp.clip(kc - c + NA_KW - 1, 0, 2 * NA_KW - 2)
    win = jnp.clip(c - NA_KW // 2, 0, GRID_W - NA_KW)
    ok = (kc >= win) & (kc < win + NA_KW)
    onehot = (col_off[..., None] == jnp.arange(2 * NA_KW - 1)).astype(F32)
    cols = jnp.einsum("hrv,ckv->hrck", rpb.astype(F32), onehot, precision=lax.Precision.HIGHEST)
    cols = jnp.where(ok, cols, NEG_INF)
    v = jnp.arange(3)[:, None, None]
    g = jnp.arange(grp)[None, :, None]
    ru = jnp.arange(span)[None, None, :]
    u0 = jnp.where(v == 0, 0, jnp.where(v == 1, kh // 2, rows - span))
    first = jnp.clip(u0 + (kh // 2) * v + g - kh // 2, 0, rows - kh) - u0
    valid = (ru >= first) & (ru < first + kh)
    row_off = jnp.clip(ru - (kh // 2) * v - g + kh - 1, 0, 2 * kh - 2)
    tab = jnp.where(valid[None, :, :, :, None, None], cols[:, row_off], NEG_INF)
    return tab.transpose(0, 1, 2, 4, 3, 5).reshape(rpb.shape[0], 3, grp * GRID_W, span * GRID_W)


def _na_table_rows(seq):
    return min(seq // GRID_W, 3 * NA_GROUP)


def _na_kernel(q_ref, k_ref, v_ref, bias_ref, o_ref, *, rows, scale):
    kh, grp = NA_KH_MAX, NA_GROUP
    span = min(2 * grp, rows)

    def group_body(gi, carry):
        i0 = gi * grp
        u0 = jnp.clip(i0 - kh // 2, 0, rows - span)
        qoff = pl.multiple_of(i0 * GRID_W, grp * GRID_W)
        koff = pl.multiple_of(u0 * GRID_W, (kh // 2) * GRID_W)
        q = q_ref[pl.ds(qoff, grp * GRID_W), :]
        kw = k_ref[pl.ds(koff, span * GRID_W), :]
        vw = v_ref[pl.ds(koff, span * GRID_W), :]
        s = lax.dot_general(q, kw, (((1,), (1,)), ((), ())), preferred_element_type=F32)
        s = s * scale + bias_ref[(i0 - u0) // (kh // 2)]
        m = jnp.max(s, axis=-1, keepdims=True)
        p = jnp.exp(s - m)
        l = jnp.sum(p, axis=-1, keepdims=True)
        o = jnp.dot(p.astype(BF16), vw, preferred_element_type=F32) / l
        o_ref[pl.ds(qoff, grp * GRID_W), :] = o.astype(o_ref.dtype)
        return carry

    lax.fori_loop(0, rows // grp, group_body, 0, unroll=2)


def _na_attention(proj, bias, *, row0, batch, seq, heads, col_q, col_k, col_v):
    hd = LANES
    rows = seq // GRID_W
    rb = row0 // seq
    assert row0 % seq == 0 and rows % NA_GROUP == 0 and rows >= NA_KH_MAX
    spec = lambda col: pl.BlockSpec((seq, hd), lambda b, h: (rb + b, col // hd + h))
    return pl.pallas_call(
        functools.partial(_na_kernel, rows=rows, scale=hd ** -0.5),
        grid=(batch, heads),
        in_specs=[spec(col_q), spec(col_k), spec(col_v),
                  pl.BlockSpec((None,) + bias.shape[1:], lambda b, h: (h, 0, 0, 0))],
        out_specs=pl.BlockSpec((seq, hd), lambda b, h: (b, h)),
        out_shape=jax.ShapeDtypeStruct((batch * seq, heads * hd), BF16),
        compiler_params=_cparams("parallel", "parallel"),
        name="na_attention",
    )(proj, proj, proj, bias)


def _lru_kernel(x_ref, y_ref, cw_ref, cb_ref, wa_ref, ba_ref, wx_ref, bx_ref, ls_ref, o_ref,
                a0_ref, b0_ref, a1_ref, b1_ref, *, seq):
    x = x_ref[...].astype(F32)
    t_idx = lax.broadcasted_iota(I32, x.shape, 0)
    xc = (cw_ref[0:1, :] * jnp.where(t_idx >= 2, pltpu.roll(x, 2, 0), 0.0)
          + cw_ref[1:2, :] * jnp.where(t_idx >= 1, pltpu.roll(x, 1, 0), 0.0)
          + cw_ref[2:3, :] * x
          + cw_ref[3:4, :] * jnp.where(t_idx < seq - 1, pltpu.roll(x, seq - 1, 0), 0.0)
          + cb_ref[...])
    xb = xc.astype(BF16)
    for d, (a_ref, b_ref) in enumerate(((a0_ref, b0_ref), (a1_ref, b1_ref))):
        r = jax.nn.sigmoid(jnp.dot(xb, wa_ref[d], preferred_element_type=F32) + ba_ref[d:d + 1, :])
        ig = jax.nn.sigmoid(jnp.dot(xb, wx_ref[d], preferred_element_type=F32) + bx_ref[d:d + 1, :])
        a = jnp.exp(ls_ref[d:d + 1, :] * r)
        a_ref[...] = a
        b_ref[...] = jnp.sqrt(1.0 - a * a) * (ig * xc)

    n_tiles = seq // SUBLANES
    row = lax.broadcasted_iota(I32, (SUBLANES, LANES), 0)

    def tile_scan(a, b, carry, reverse):
        for s in (1, 2, 4):
            if reverse:
                keep = row < SUBLANES - s
                shift = SUBLANES - s
            else:
                keep = row >= s
                shift = s
            a_sh = jnp.where(keep, pltpu.roll(a, shift, 0), 1.0)
            b_sh = jnp.where(keep, pltpu.roll(b, shift, 0), 0.0)
            b = a * b_sh + b
            a = a * a_sh
        e = 0 if reverse else SUBLANES - 1
        a_edge = jnp.broadcast_to(a[e:e + 1, :], (SUBLANES, LANES))
        b_edge = jnp.broadcast_to(b[e:e + 1, :], (SUBLANES, LANES))
        return b + a * carry, b_edge + a_edge * carry

    def scan_body(v, carry):
        cf, cb = carry
        rf = pl.multiple_of(v * SUBLANES, SUBLANES)
        rb = pl.multiple_of((n_tiles - 1 - v) * SUBLANES, SUBLANES)
        hf, cf = tile_scan(a0_ref[pl.ds(rf, SUBLANES), :], b0_ref[pl.ds(rf, SUBLANES), :], cf, False)
        hb, cb = tile_scan(a1_ref[pl.ds(rb, SUBLANES), :], b1_ref[pl.ds(rb, SUBLANES), :], cb, True)
        b0_ref[pl.ds(rf, SUBLANES), :] = hf
        b1_ref[pl.ds(rb, SUBLANES), :] = hb
        return cf, cb

    zeros = jnp.zeros((SUBLANES, LANES), F32)
    lax.fori_loop(0, n_tiles, scan_body, (zeros, zeros), unroll=4)

    h = b0_ref[...] + b1_ref[...]
    y = y_ref[...].astype(F32)
    o_ref[...] = (h * jax.nn.gelu(y, approximate=True)).astype(o_ref.dtype)


def _rglru(proj, conv_w, conv_b, wa, ba, wx, bx, log_scale, *, row0, batch, seq, col_x, col_y):
    nb = wa.shape[1]
    bw = wa.shape[2]
    assert bw == LANES and row0 % seq == 0 and seq % (4 * SUBLANES) == 0
    rb = row0 // seq
    cvec = lambda rows_: pl.BlockSpec((rows_, bw), lambda b, n: (0, n))
    wspec = pl.BlockSpec((2, None, bw, bw), lambda b, n: (0, n, 0, 0))
    return pl.pallas_call(
        functools.partial(_lru_kernel, seq=seq),
        grid=(batch, nb),
        in_specs=[pl.BlockSpec((seq, bw), lambda b, n: (rb + b, col_x // bw + n)),
                  pl.BlockSpec((seq, bw), lambda b, n: (rb + b, col_y // bw + n)),
                  cvec(conv_w.shape[0]), cvec(1), wspec, cvec(2), wspec, cvec(2), cvec(2)],
        out_specs=pl.BlockSpec((seq, bw), lambda b, n: (b, n)),
        out_shape=jax.ShapeDtypeStruct((batch * seq, nb * bw), BF16),
        scratch_shapes=[pltpu.VMEM((seq, bw), F32)] * 4,
        compiler_params=_cparams("parallel", "parallel"),
        name="rglru",
    )(proj, proj, conv_w, conv_b, wa, ba, wx, bx, log_scale)


def _diff_kernel(sc_ref, q_ref, k_ref, v_ref, g_ref, o_ref, *, tq, seq, dc, scale):
    h = pl.program_id(1)
    lam = sc_ref[0]
    out_scale = sc_ref[1]
    slope = sc_ref[2 + h]
    q = q_ref[...]
    k = k_ref[...]
    v = v_ref[...]
    lane = lax.broadcasted_iota(I32, q.shape, 1)
    qpos = pl.program_id(2) * tq + lax.broadcasted_iota(I32, (tq, 1), 0)
    kpos = lax.broadcasted_iota(I32, (1, seq), 1)
    bias = jnp.abs(qpos - kpos).astype(F32) * (-slope)
    outs = []
    for m_idx in range(2):
        in_map = (lane < dc) if m_idx == 0 else (lane >= dc)
        qm = jnp.where(in_map, q, jnp.zeros_like(q)) * jnp.asarray(scale, q.dtype)
        s = lax.dot_general(qm, k, (((1,), (1,)), ((), ())), preferred_element_type=F32) + bias
        mx = jnp.max(s, axis=-1, keepdims=True)
        p = jnp.exp(s - mx)
        l = jnp.sum(p, axis=-1, keepdims=True)
        outs.append(jnp.dot(p.astype(BF16), v, preferred_element_type=F32) / l)
    o = outs[0] - lam * outs[1]
    o = o * lax.rsqrt(jnp.mean(o * o, axis=-1, keepdims=True) + LN_EPS)
    o_ref[...] = (o * g_ref[...] * out_scale).astype(o_ref.dtype)


def _diff_attention(proj, scalars, subln_g, *, row0, batch, seq, heads, dc, col_q, col_k, col_v, tq):
    hd = 2 * dc
    assert hd == LANES and row0 % seq == 0 and seq % tq == 0
    assert math.log2(dc) % 2 == 0
    nq = seq // tq
    rb = row0 // seq
    return pl.pallas_call(
        functools.partial(_diff_kernel, tq=tq, seq=seq, dc=dc, scale=dc ** -0.5),
        grid=(batch, heads, nq),
        in_specs=[pl.BlockSpec(memory_space=pltpu.SMEM),
                  pl.BlockSpec((tq, hd), lambda b, h, i: ((rb + b) * nq + i, col_q // hd + h)),
                  pl.BlockSpec((seq, hd), lambda b, h, i: (rb + b, col_k // hd + h)),
                  pl.BlockSpec((seq, hd), lambda b, h, i: (rb + b, col_v // hd + h)),
                  pl.BlockSpec((1, hd), lambda b, h, i: (0, 0))],
        out_specs=pl.BlockSpec((tq, hd), lambda b, h, i: (b * nq + i, h)),
        out_shape=jax.ShapeDtypeStruct((batch * seq, heads * hd), BF16),
        compiler_params=_cparams("parallel", "parallel", "arbitrary"),
        name="diff_attention",
    )(scalars, proj, proj, proj, subln_g)


def _layer_norm(y, g, b):
    mu = jnp.mean(y, axis=-1, keepdims=True)
    yc = y - mu
    var = jnp.mean(yc * yc, axis=-1, keepdims=True)
    return yc * lax.rsqrt(var + LN_EPS) * g + b


def _packed_rows(d):
    return d // (2 * LANES)


def _store_packed(ref, y):
    m, d = y.shape
    pr = _packed_rows(d)
    lo = lax.bitcast_convert_type(y[:, :d // 2].astype(BF16).astype(F32), jnp.uint32)
    hi = lax.bitcast_convert_type(y[:, d // 2:].astype(BF16).astype(F32), jnp.uint32)
    word = hi | (lo >> 16)
    for c in range(pr):
        ref[pl.ds(c, m, stride=pr), :] = word[:, c * LANES:(c + 1) * LANES]


def _load_packed(ref, c, m, pr, row0=0):
    word = ref[pl.ds(row0 * pr + c, m, stride=pr), :]
    lo = lax.bitcast_convert_type(word << 16, F32)
    hi = lax.bitcast_convert_type(word & jnp.uint32(0xFFFF0000), F32)
    return lo, hi


def _merge_kernel(*refs, d, alpha, offsets):
    g = len(offsets)
    i = pl.program_id(0)
    x_refs, oa_refs, ob_refs, oc_refs = (refs[j * g:(j + 1) * g] for j in range(4))
    g_ref, wa_ref, wb_ref, wc_ref, wo_ref, lg_ref, lb_ref, o_ref, op_ref = refs[4 * g:]
    x = _stream_tile(x_refs, offsets, i)
    merged = None
    for idx, (o_refs, w_br) in enumerate(((oa_refs, wa_ref), (ob_refs, wb_ref), (oc_refs, wc_ref))):
        gate = jax.nn.sigmoid(g_ref[:, idx * d:(idx + 1) * d].astype(F32))
        term = gate * jnp.dot(_stream_tile(o_refs, offsets, i), w_br[...], preferred_element_type=F32)
        merged = term if merged is None else merged + term
    y = alpha * x + jnp.dot(merged.astype(BF16), wo_ref[...], preferred_element_type=F32)
    y = _layer_norm(y, lg_ref[...], lb_ref[...])
    o_ref[...] = y
    _store_packed(op_ref, y)


def _merge_out_ln(xs, proj, oas, obs, ocs, wa, wb, wc, wo, ln_g, ln_b, *, alpha, tm):
    d = xs[0].shape[1]
    pr = _packed_rows(d)
    offsets, x_specs, n_blocks = _stream_specs(xs, tm)
    branch_specs = []
    for os_ in (oas, obs, ocs):
        branch_specs += _stream_specs(os_, tm)[1]
    n = n_blocks * tm
    row = lambda width: pl.BlockSpec((tm, width), lambda i: (i, 0))
    full = lambda arr: _resident(arr.shape, lambda i: (0, 0))
    return pl.pallas_call(
        functools.partial(_merge_kernel, d=d, alpha=alpha, offsets=offsets),
        grid=(n_blocks,),
        in_specs=x_specs + branch_specs + [row(3 * d), full(wa), full(wb), full(wc), full(wo), full(ln_g),
                                           full(ln_b)],
        out_specs=[row(d), pl.BlockSpec((tm * pr, LANES), lambda i: (i, 0))],
        out_shape=[jax.ShapeDtypeStruct((n, d), F32), jax.ShapeDtypeStruct((n * pr, LANES), jnp.uint32)],
        compiler_params=_cparams("parallel"),
        name="merge_out_ln1",
    )(*xs, *oas, *obs, *ocs, proj, wa, wb, wc, wo, ln_g, ln_b)


def _router_kernel(x_ref, w_ref, b_ref, idx_ref, wgt_ref, rank_ref, cnt_ref, carry_ref, *, tt, n_exp):
    @pl.when(pl.program_id(0) == 0)
    def _():
        carry_ref[...] = jnp.zeros_like(carry_ref)

    logits = jnp.dot(x_ref[...], w_ref[...], preferred_element_type=F32,
                     precision=lax.Precision.HIGHEST) + b_ref[...]
    lane = lax.broadcasted_iota(I32, (tt, n_exp), 1).astype(F32)
    work = logits
    vals, idxs, hots = [], [], []
    for _ in range(TOP_K):
        m = jnp.max(work, axis=-1, keepdims=True)
        idx = jnp.min(jnp.where(work == m, lane, float(n_exp)), axis=-1, keepdims=True)
        hot = lane == idx
        vals.append(m)
        idxs.append(idx)
        hots.append(hot)
        work = jnp.where(hot, -jnp.inf, work)
    exps = [jnp.exp(v - vals[0]) for v in vals]
    denom = exps[0]
    for e in exps[1:]:
        denom = denom + e
    onehot = jnp.zeros((tt, n_exp), F32)
    for hot in hots:
        onehot = onehot + hot.astype(F32)
    r_i = lax.broadcasted_iota(I32, (tt, tt), 0)
    c_i = lax.broadcasted_iota(I32, (tt, tt), 1)
    lower = jnp.where(r_i > c_i, 1.0, 0.0).astype(BF16)
    before = jnp.dot(lower, onehot.astype(BF16), preferred_element_type=F32) + carry_ref[...]
    for k in range(TOP_K):
        idx_ref[:, k:k + 1] = idxs[k].astype(I32)
        wgt_ref[:, k:k + 1] = exps[k] / denom
        rank_ref[:, k:k + 1] = jnp.sum(jnp.where(hots[k], before, 0.0), axis=-1, keepdims=True).astype(I32)
    carry_ref[...] = carry_ref[...] + jnp.sum(onehot, axis=0, keepdims=True)
    cnt_ref[...] = carry_ref[...].astype(I32)


def _router(x, w_router, b_router, *, tt):
    n, d = x.shape
    n_exp = w_router.shape[1]
    out4 = lambda dt: jax.ShapeDtypeStruct((n, TOP_K), dt)
    spec4 = pl.BlockSpec((tt, TOP_K), lambda i: (i, 0))
    return pl.pallas_call(
        functools.partial(_router_kernel, tt=tt, n_exp=n_exp),
        grid=(n // tt,),
        in_specs=[pl.BlockSpec((tt, d), lambda i: (i, 0)),
                  _resident((d, n_exp), lambda i: (0, 0)),
                  _resident((1, n_exp), lambda i: (0, 0))],
        out_specs=[spec4, spec4, spec4, pl.BlockSpec((1, n_exp), lambda i: (0, 0))],
        out_shape=[out4(I32), out4(F32), out4(I32), jax.ShapeDtypeStruct((1, n_exp), I32)],
        scratch_shapes=[pltpu.VMEM((1, n_exp), F32)],
        compiler_params=_cparams("arbitrary"),
        name="moe_router",
    )(x, w_router, b_router)


def _w1_prep_kernel(w_ref, o_ref):
    grp = 2 * LANES
    r = lax.broadcasted_iota(I32, (grp, grp), 0)
    c = lax.broadcasted_iota(I32, (grp, grp), 1)
    src = jnp.where(c < LANES, 2 * c, 2 * (c - LANES) + 1)
    perm = jnp.where(r == src, 1.0, 0.0).astype(BF16)
    for g in range(w_ref.shape[1] // grp):
        blk = w_ref[:, g * grp:(g + 1) * grp].astype(BF16)
        o_ref[:, g * grp:(g + 1) * grp] = jnp.dot(blk, perm, preferred_element_type=F32).astype(BF16)


def _w1_prep(w1, layer, *, td, tc):
    _, n_exp, d, two_f = w1.shape
    return pl.pallas_call(
        _w1_prep_kernel,
        grid=(n_exp, d // td, two_f // tc),
        in_specs=[pl.BlockSpec((None, None, td, tc), lambda e, i, j: (layer, e, i, j))],
        out_specs=pl.BlockSpec((None, td, tc), lambda e, i, j: (e, i, j)),
        out_shape=jax.ShapeDtypeStruct(w1.shape[1:], BF16),
        compiler_params=_cparams("parallel", "parallel", "parallel"),
        name="w1_prep",
    )(w1)


def _expert_kernel(te_ref, tv_ref, rt0_ref, rt1_ref, rta_ref, xp_ref, w1_ref, b1g_ref, b1l_ref, w2_ref, b2_ref,
                   o_ref, xg_ref, sem_ref, xb_ref, acc_ref, *, tm, tf, d, nf):
    i = pl.program_id(0)
    f = pl.program_id(1)
    last = nf - 1
    pr = _packed_rows(d)
    n_slots = EXPERT_LOOKAHEAD + 1
    slot = i % n_slots
    live = tv_ref[i] > 0
    fetched = (i < EXPERT_LOOKAHEAD) | (tv_ref[jnp.maximum(i - EXPERT_LOOKAHEAD, 0)] > 0)

    def row_copy(tok_ref, r, slot_):
        src = xp_ref.at[pl.ds(pl.multiple_of(tok_ref[r] * pr, pr), pr)]
        dst = xg_ref.at[slot_, pl.ds(pl.multiple_of(r * pr, pr), pr)]
        return pltpu.make_async_copy(src, dst, sem_ref.at[slot_])

    @pl.when((i == 0) & (f == 0))
    def _():
        for tile, tok_ref in enumerate((rt0_ref, rt1_ref)):
            def body(r, carry, tile=tile, tok_ref=tok_ref):
                row_copy(tok_ref, r, tile).start()
                return carry
            lax.fori_loop(0, tm, body, 0, unroll=8)

    @pl.when(fetched & (f == 0))
    def _():
        pltpu.make_async_copy(xg_ref.at[slot], xg_ref.at[slot], sem_ref.at[slot]).wait()

    @pl.when(live & (f == 0))
    def _():
        for c in range(pr):
            lo, hi = _load_packed(xg_ref.at[slot], c, tm, pr)
            xb_ref[:, c * LANES:(c + 1) * LANES] = lo.astype(BF16)
            xb_ref[:, d // 2 + c * LANES:d // 2 + (c + 1) * LANES] = hi.astype(BF16)
        acc_ref[...] = jnp.zeros_like(acc_ref)

    @pl.when(live)
    def _():
        chunk = tm // nf
        ahead_slot = (i + EXPERT_LOOKAHEAD) % n_slots
        for r in range(chunk):
            row_copy(rta_ref, f * chunk + r, ahead_slot).start()
        h = jnp.dot(xb_ref[...], w1_ref[...], preferred_element_type=F32)
        acts = []
        for j in range(tf // LANES):
            hg = h[:, 2 * j * LANES:(2 * j + 1) * LANES] + b1g_ref[:, j * LANES:(j + 1) * LANES]
            hl = h[:, (2 * j + 1) * LANES:(2 * j + 2) * LANES] + b1l_ref[:, j * LANES:(j + 1) * LANES]
            glu = jnp.minimum(hg, SWIGLU_LIMIT)
            lin = jnp.clip(hl, -SWIGLU_LIMIT, SWIGLU_LIMIT)
            acts.append((glu * jax.nn.sigmoid(SWIGLU_ALPHA * glu) * (lin + 1.0)).astype(BF16))
        act = jnp.concatenate(acts, axis=1)
        acc_ref[...] += jnp.dot(act, w2_ref[...], preferred_element_type=F32)

    @pl.when(live & (f == last))
    def _():
        _store_packed(o_ref, acc_ref[...] + b2_ref[...])

    @pl.when(jnp.logical_not(live) & (f == last))
    def _():
        o_ref[...] = jnp.zeros_like(o_ref)


def _expert_mlp(xp, row_tok, tile_expert, tile_valid, w1p, b1g, b1l, w2, layer, b2, *, tm, tf):
    d = w2.shape[3]
    pr = _packed_rows(d)
    n_tiles = row_tok.shape[0] // tm
    nf = w2.shape[2] // tf
    fidx = lambda f, tv, i: jnp.where(tv[i] > 0, f, nf - 1)
    grid_spec = pltpu.PrefetchScalarGridSpec(
        num_scalar_prefetch=2,
        grid=(n_tiles, nf),
        in_specs=[pl.BlockSpec((tm,), lambda i, f, te, tv: (0,), memory_space=pltpu.SMEM),
                  pl.BlockSpec((tm,), lambda i, f, te, tv: (1,), memory_space=pltpu.SMEM),
                  pl.BlockSpec((tm,), lambda i, f, te, tv: (jnp.minimum(i + EXPERT_LOOKAHEAD, n_tiles - 1),),
                               memory_space=pltpu.SMEM),
                  pl.BlockSpec(memory_space=pl.ANY),
                  pl.BlockSpec((None, d, 2 * tf), lambda i, f, te, tv: (te[i], 0, fidx(f, tv, i))),
                  pl.BlockSpec((None, 1, tf), lambda i, f, te, tv: (te[i], 0, fidx(f, tv, i))),
                  pl.BlockSpec((None, 1, tf), lambda i, f, te, tv: (te[i], 0, fidx(f, tv, i))),
                  pl.BlockSpec((None, None, tf, d), lambda i, f, te, tv: (layer, te[i], fidx(f, tv, i), 0)),
                  pl.BlockSpec((None, 1, d), lambda i, f, te, tv: (te[i], 0, 0))],
        out_specs=pl.BlockSpec((tm * pr, LANES), lambda i, f, te, tv: (i, 0)),
        scratch_shapes=[pltpu.VMEM((EXPERT_LOOKAHEAD + 1, tm * pr, LANES), jnp.uint32),
                        pltpu.SemaphoreType.DMA((EXPERT_LOOKAHEAD + 1,)),
                        pltpu.VMEM((tm, d), BF16), pltpu.VMEM((tm, d), F32)],
    )
    return pl.pallas_call(
        functools.partial(_expert_kernel, tm=tm, tf=tf, d=d, nf=nf),
        grid_spec=grid_spec,
        out_shape=jax.ShapeDtypeStruct((n_tiles * tm * pr, LANES), jnp.uint32),
        compiler_params=_cparams("arbitrary", "arbitrary"),
        name="expert_mlp",
    )(tile_expert, tile_valid, row_tok, row_tok, row_tok, xp, w1p, b1g, b1l, w2, b2)


def _combine_kernel(dc_ref, dn_ref, x_ref, yp_ref, w_ref, lg_ref, lb_ref, o_ref, yg_ref, sem_ref, ysc_ref,
                    *, alpha, tt, d):
    i = pl.program_id(0)
    n_steps = pl.num_programs(0)
    pr = _packed_rows(d)
    slot = i % 2

    def start_rows(dest_ref, t, slot_):
        for k in range(TOP_K):
            src = yp_ref.at[pl.ds(pl.multiple_of(dest_ref[t * TOP_K + k] * pr, pr), pr)]
            dst = yg_ref.at[slot_, pl.ds(pl.multiple_of((k * tt + t) * pr, pr), pr)]
            pltpu.make_async_copy(src, dst, sem_ref.at[slot_]).start()

    def wait_slot(slot_):
        pltpu.make_async_copy(yg_ref.at[slot_], yg_ref.at[slot_], sem_ref.at[slot_]).wait()

    @pl.when(i == 0)
    def _():
        def body(t, carry):
            start_rows(dc_ref, t, 0)
            return carry
        lax.fori_loop(0, tt, body, 0, unroll=2)

    wait_slot(slot)

    tokens_per_group = tt // pr
    for c in range(pr):
        for t in range(c * tokens_per_group, (c + 1) * tokens_per_group):
            start_rows(dn_ref, t, 1 - slot)
        lo_cols = slice(c * LANES, (c + 1) * LANES)
        hi_cols = slice(d // 2 + c * LANES, d // 2 + (c + 1) * LANES)
        acc_lo = alpha * x_ref[:, lo_cols]
        acc_hi = alpha * x_ref[:, hi_cols]
        for k in range(TOP_K):
            lo, hi = _load_packed(yg_ref.at[slot], c, tt, pr, row0=k * tt)
            wk = w_ref[:, k:k + 1]
            acc_lo = acc_lo + wk * lo
            acc_hi = acc_hi + wk * hi
        ysc_ref[:, lo_cols] = acc_lo
        ysc_ref[:, hi_cols] = acc_hi
    o_ref[...] = _layer_norm(ysc_ref[...], lg_ref[...], lb_ref[...])

    @pl.when(i == n_steps - 1)
    def _():
        wait_slot(1 - slot)


def _combine_ln(x, yp, dest, top_w, ln_g, ln_b, *, alpha, tt, row0, nrows):
    d = x.shape[1]
    pr = _packed_rows(d)
    assert row0 % tt == 0 and nrows % tt == 0
    n_steps = nrows // tt
    b0 = row0 // tt
    return pl.pallas_call(
        functools.partial(_combine_kernel, alpha=alpha, tt=tt, d=d),
        grid=(n_steps,),
        in_specs=[pl.BlockSpec((tt * TOP_K,), lambda i: (b0 + i,), memory_space=pltpu.SMEM),
                  pl.BlockSpec((tt * TOP_K,), lambda i: (b0 + jnp.minimum(i + 1, n_steps - 1),),
                               memory_space=pltpu.SMEM),
                  pl.BlockSpec((tt, d), lambda i: (b0 + i, 0)),
                  pl.BlockSpec(memory_space=pl.ANY),
                  pl.BlockSpec((tt, TOP_K), lambda i: (b0 + i, 0)),
                  _resident((1, d), lambda i: (0, 0)),
                  _resident((1, d), lambda i: (0, 0))],
        out_specs=pl.BlockSpec((tt, d), lambda i: (i, 0)),
        out_shape=jax.ShapeDtypeStruct((nrows, d), F32),
        scratch_shapes=[pltpu.VMEM((2, TOP_K * tt * pr, LANES), jnp.uint32), pltpu.SemaphoreType.DMA((2,)),
                        pltpu.VMEM((tt, d), F32)],
        compiler_params=_cparams("arbitrary"),
        name="combine_ln2",
    )(dest, dest, x, yp, top_w, ln_g, ln_b)


def _moe_layout(top_i, rank, counts, *, tm):
    n, k = top_i.shape
    n_exp = counts.shape[0]
    n_tiles = (n * k) // tm + n_exp + EXPERT_LOOKAHEAD
    padded = (counts + tm - 1) // tm * tm
    pends = jnp.cumsum(padded)
    pstarts = pends - padded
    first_row = jnp.sum(jnp.where(top_i[..., None] == jnp.arange(n_exp, dtype=I32), pstarts.astype(I32), 0), -1)
    dest = first_row + rank
    tile_start = jnp.arange(n_tiles, dtype=I32) * tm
    tile_expert = jnp.clip(jnp.searchsorted(pends, tile_start, side="right"), 0, n_exp - 1).astype(I32)
    tile_valid = jnp.clip(counts[tile_expert] - (tile_start - pstarts[tile_expert]), 0, tm).astype(I32)
    tok = jnp.broadcast_to(jnp.arange(n, dtype=I32)[:, None], (n, k))
    row_tok = jnp.zeros((n_tiles * tm,), I32).at[dest.reshape(-1)].set(tok.reshape(-1))
    return dest, row_tok, tile_expert, tile_valid


def _moe(x, xp, p, *, tm, tf, tt_router):
    top_i, top_w, rank, counts = _router(x, p["w_router"], p["b_router"], tt=tt_router)
    dest, row_tok, tile_expert, tile_valid = _moe_layout(top_i, rank, counts[0], tm=tm)
    yp = _expert_mlp(xp, row_tok, tile_expert, tile_valid, p["w1p"], p["b1g"], p["b1l"], p["w2"], p["layer"],
                     p["b2"], tm=tm, tf=tf)
    return yp, dest.reshape(-1), top_w


def _prepare_params(w_in, rpb_a, conv_w, conv_b, lru_wa, lru_ba, lru_wx, lru_bx, lru_lambda, lam_q1, lam_k1,
                    lam_q2, lam_k2, subln_g, w_proj_a, w_proj_b, w_proj_c, w_out, ln1_g, ln1_b, w_router,
                    b_router, w1, b1, w2, b2, ln2_g, ln2_b, *, d_model, na_rows, diff_heads, w1_prep_tiles):
    depth = w_in.shape[0]
    gate0 = w_in.shape[2] - 3 * d_model
    w2_bf16 = w2.astype(BF16)
    layers = []
    for l in range(depth):
        lam_init = 0.8 - 0.6 * math.exp(-0.3 * l)
        lam = (jnp.exp(jnp.sum(lam_q1[l].astype(F32) * lam_k1[l].astype(F32)))
               - jnp.exp(jnp.sum(lam_q2[l].astype(F32) * lam_k2[l].astype(F32))) + lam_init)
        slopes = 2.0 ** (-8.0 * (jnp.arange(diff_heads, dtype=F32) + 1.0) / diff_heads)
        layers.append(dict(
            w_in=jnp.concatenate([w_in[l, :, gate0:], w_in[l, :, :gate0]], axis=1).astype(BF16),
            na_bias={r: _na_bias_table(rpb_a[l], r) for r in na_rows},
            conv_w=conv_w[l], conv_b=conv_b[l][None, :],
            lru_wa=lru_wa[l].astype(BF16), lru_ba=lru_ba[l], lru_wx=lru_wx[l].astype(BF16), lru_bx=lru_bx[l],
            lru_log_scale=-LRU_C * jax.nn.softplus(-lru_lambda[l].astype(F32)),
            diff_scalars=jnp.concatenate([jnp.stack([lam, jnp.asarray(1.0 - lam_init, F32)]), slopes]).astype(F32),
            subln_g=subln_g[l][None, :],
            w_proj_a=w_proj_a[l].astype(BF16), w_proj_b=w_proj_b[l].astype(BF16),
            w_proj_c=w_proj_c[l].astype(BF16), w_out=w_out[l].astype(BF16),
            ln1_g=ln1_g[l][None, :], ln1_b=ln1_b[l][None, :],
            w_router=w_router[l], b_router=b_router[l][None, :],
            w1p=_w1_prep(w1, l, td=w1_prep_tiles[0], tc=w1_prep_tiles[1]),
            b1g=b1[l, :, None, 0::2], b1l=b1[l, :, None, 1::2],
            w2=w2_bf16, layer=l, b2=b2[l][:, None, :],
            ln2_g=ln2_g[l][None, :], ln2_b=ln2_b[l][None, :],
        ))
    return layers


def _encoder_layer(xs, p, groups, *, alpha, cfg):
    d = xs[0].shape[1]
    proj = _in_proj(xs, p["w_in"], cfg["proj_tm"], cfg["proj_tn"])
    na_w = cfg["na_heads"] * LANES
    rnn_w = p["lru_ba"].shape[1]
    dq = cfg["diff_heads"] * 2 * cfg["diff_dc"]
    c_qa = 3 * d
    c_xb = c_qa + 3 * na_w
    c_qc = c_xb + 2 * rnn_w
    oa, ob, oc = [], [], []
    for row0, batch, seq in groups:
        oa.append(_na_attention(proj, p["na_bias"][_na_table_rows(seq)], row0=row0, batch=batch, seq=seq, heads=cfg["na_heads"],
                                col_q=c_qa, col_k=c_qa + na_w, col_v=c_qa + 2 * na_w))
        ob.append(_rglru(proj, p["conv_w"], p["conv_b"], p["lru_wa"], p["lru_ba"], p["lru_wx"], p["lru_bx"],
                         p["lru_log_scale"], row0=row0, batch=batch, seq=seq, col_x=c_xb, col_y=c_xb + rnn_w))
        oc.append(_diff_attention(proj, p["diff_scalars"], p["subln_g"], row0=row0, batch=batch, seq=seq,
                                  heads=cfg["diff_heads"], dc=cfg["diff_dc"], col_q=c_qc, col_k=c_qc + dq,
                                  col_v=c_qc + 2 * dq, tq=cfg["diff_tq"]))
    x1, x1p = _merge_out_ln(xs, proj, oa, ob, oc, p["w_proj_a"], p["w_proj_b"], p["w_proj_c"], p["w_out"],
                            p["ln1_g"], p["ln1_b"], alpha=alpha, tm=cfg["merge_tm"])
    yp, dest, top_w = _moe(x1, x1p, p, tm=cfg["moe_tm"], tf=cfg["moe_tf"], tt_router=cfg["router_tt"])
    return tuple(_combine_ln(x1, yp, dest, top_w, p["ln2_g"], p["ln2_b"], alpha=alpha, tt=cfg["combine_tt"],
                             row0=row0, nrows=batch * seq) for row0, batch, seq in groups)


DEFAULT_CFG = dict(proj_tm=1024, proj_tn=1024, na_heads=4, diff_heads=4, diff_dc=64, diff_tq=256, merge_tm=256,
                   moe_tm=512, moe_tf=1024, router_tt=512, combine_tt=256, w1_prep_tiles=(512, 2048))


def _trunk(xs, params, cfg):
    d = xs[0].shape[-1]
    groups, row0 = [], 0
    for x in xs:
        groups.append((row0, x.shape[0], x.shape[1]))
        row0 += x.shape[0] * x.shape[1]
    na_rows = sorted({_na_table_rows(s) for _, _, s in groups})
    layers = _prepare_params(*params, d_model=d, na_rows=na_rows, diff_heads=cfg["diff_heads"],
                             w1_prep_tiles=cfg["w1_prep_tiles"])
    alpha = (2 * len(layers)) ** 0.25
    streams = tuple(x.reshape(-1, d) for x in xs)
    for p in layers:
        streams = _encoder_layer(streams, p, tuple(groups), alpha=alpha, cfg=cfg)
    return tuple(o.reshape(xin.shape) for o, xin in zip(streams, xs))


def kernel(x_prompt, x_sample, w_in, rpb_a, conv_w, conv_b, lru_wa, lru_ba, lru_wx, lru_bx, lru_lambda, lam_q1,
           lam_k1, lam_q2, lam_k2, subln_g, w_proj_a, w_proj_b, w_proj_c, w_out, ln1_g, ln1_b, w_router,
           b_router, w1, b1, w2, b2, ln2_g, ln2_b):
    params = (w_in, rpb_a, conv_w, conv_b, lru_wa, lru_ba, lru_wx, lru_bx, lru_lambda, lam_q1, lam_k1, lam_q2,
              lam_k2, subln_g, w_proj_a, w_proj_b, w_proj_c, w_out, ln1_g, ln1_b, w_router, b_router, w1, b1,
              w2, b2, ln2_g, ln2_b)
    return _trunk((x_prompt, x_sample), params, DEFAULT_CFG)
```

```python
import functools
import math

import jax
import jax.numpy as jnp
from jax import lax
from jax.experimental import pallas as pl
from jax.experimental.pallas import tpu as pltpu

F32 = jnp.float32
BF16 = jnp.bfloat16
I32 = jnp.int32

GRID_W = 64
NA_KH_MAX = 8
NA_KW = 16
NA_GROUP = 8
EXPERT_LOOKAHEAD = 2
LRU_C = 8.0
TOP_K = 4
SWIGLU_LIMIT = 7.0
SWIGLU_ALPHA = 1.702
LN_EPS = 1e-5
NEG_INF = -1e30
LOG2E = math.log2(math.e)

LANES = 128
SUBLANES = 8
VMEM_LIMIT = 56 * 1024 * 1024


def _cparams(*sem):
    return pltpu.CompilerParams(dimension_semantics=sem, vmem_limit_bytes=VMEM_LIMIT)


def _resident(block_shape, index_map):
    return pl.BlockSpec(block_shape, index_map, pipeline_mode=pl.Buffered(1))


def _stream_specs(xs, tm):
    offsets, specs, off = [], [], 0
    for x in xs:
        assert x.shape[0] % tm == 0
        nb = x.shape[0] // tm
        offsets.append(off)
        specs.append(pl.BlockSpec((tm, x.shape[1]),
                                  lambda *a, off=off, nb=nb: (jnp.clip(a[0] - off, 0, nb - 1), 0)))
        off += nb
    return tuple(offsets), specs, off


def _stream_tile(x_refs, offsets, i):
    x = x_refs[0][...]
    for ref, off in zip(x_refs[1:], offsets[1:]):
        x = jnp.where(i >= off, ref[...], x)
    return x


def _in_proj_kernel(*refs, offsets):
    x_refs, (w_ref, o_ref, xb_ref) = refs[:len(offsets)], refs[len(offsets):]

    @pl.when(pl.program_id(1) == 0)
    def _():
        xb_ref[...] = _stream_tile(x_refs, offsets, pl.program_id(0)).astype(BF16)

    o_ref[...] = jnp.dot(xb_ref[...], w_ref[...], preferred_element_type=F32).astype(o_ref.dtype)


def _in_proj(xs, w, tm, tn):
    d = xs[0].shape[1]
    width = w.shape[1]
    offsets, x_specs, n_blocks = _stream_specs(xs, tm)
    return pl.pallas_call(
        functools.partial(_in_proj_kernel, offsets=offsets),
        grid=(n_blocks, width // tn),
        in_specs=x_specs + [pl.BlockSpec((d, tn), lambda i, j: (0, j))],
        out_specs=pl.BlockSpec((tm, tn), lambda i, j: (i, j)),
        out_shape=jax.ShapeDtypeStruct((n_blocks * tm, width), BF16),
        scratch_shapes=[pltpu.VMEM((tm, d), BF16)],
        compiler_params=_cparams("parallel", "arbitrary"),
        name="in_proj",
    )(*xs, w)


def _na_bias_table(rpb, rows):
    kh, grp = NA_KH_MAX, NA_GROUP
    span = min(2 * grp, rows)
    c = jnp.arange(GRID_W)[:, None]
    kc = jnp.arange(GRID_W)[None, :]
    col_off = jnp.clip(kc - c + NA_KW - 1, 0, 2 * NA_KW - 2)
    win = jnp.clip(c - NA_KW // 2, 0, GRID_W - NA_KW)
    ok = (kc >= win) & (kc < win + NA_KW)
    onehot = (col_off[..., None] == jnp.arange(2 * NA_KW - 1)).astype(F32)
    cols = jnp.einsum("hrv,ckv->hrck", rpb.astype(F32), onehot, precision=lax.Precision.HIGHEST)
    cols = jnp.where(ok, cols, NEG_INF)
    v = jnp.arange(3)[:, None, None]
    g = jnp.arange(grp)[None, :, None]
    ru = jnp.arange(span)[None, None, :]
    u0 = jnp.where(v == 0, 0, jnp.where(v == 1, kh // 2, rows - span))
    first = jnp.clip(u0 + (kh // 2) * v + g - kh // 2, 0, rows - kh) - u0
    valid = (ru >= first) & (ru < first + kh)
    row_off = jnp.clip(ru - (kh // 2) * v - g + kh - 1, 0, 2 * kh - 2)
    tab = jnp.where(valid[None, :, :, :, None, None], cols[:, row_off], NEG_INF)
    tab = tab * LOG2E
    return tab.transpose(0, 1, 2, 4, 3, 5).reshape(rpb.shape[0], 3, grp * GRID_W, span * GRID_W)


def _na_table_rows(seq):
    return min(seq // GRID_W, 3 * NA_GROUP)


def _na_kernel(q_ref, k_ref, v_ref, bias_ref, o_ref, *, rows, scale):
    kh, grp = NA_KH_MAX, NA_GROUP
    span = min(2 * grp, rows)

    def group_body(gi, carry):
        i0 = gi * grp
        u0 = jnp.clip(i0 - kh // 2, 0, rows - span)
        qoff = pl.multiple_of(i0 * GRID_W, grp * GRID_W)
        koff = pl.multiple_of(u0 * GRID_W, (kh // 2) * GRID_W)
        q = q_ref[pl.ds(qoff, grp * GRID_W), :]
        kw = k_ref[pl.ds(koff, span * GRID_W), :]
        vw = v_ref[pl.ds(koff, span * GRID_W), :]
        s = lax.dot_general(q, kw, (((1,), (1,)), ((), ())), preferred_element_type=F32)
        s = s * (scale * LOG2E) + bias_ref[(i0 - u0) // (kh // 2)]
        m = jnp.max(s, axis=-1, keepdims=True)
        p = jnp.exp2(s - m)
        pv = jnp.dot(p.astype(BF16), _with_ones_column(vw), preferred_element_type=F32)
        o = pv[:, :LANES] / pv[:, LANES:LANES + 1]
        o_ref[pl.ds(qoff, grp * GRID_W), :] = o.astype(o_ref.dtype)
        return carry

    lax.fori_loop(0, rows // grp, group_body, 0, unroll=2)


def _na_attention(proj, bias, *, row0, batch, seq, heads, col_q, col_k, col_v):
    hd = LANES
    rows = seq // GRID_W
    rb = row0 // seq
    assert row0 % seq == 0 and rows % NA_GROUP == 0 and rows >= NA_KH_MAX
    spec = lambda col: pl.BlockSpec((seq, hd), lambda b, h: (rb + b, col // hd + h))
    return pl.pallas_call(
        functools.partial(_na_kernel, rows=rows, scale=hd ** -0.5),
        grid=(batch, heads),
        in_specs=[spec(col_q), spec(col_k), spec(col_v),
                  pl.BlockSpec((None,) + bias.shape[1:], lambda b, h: (h, 0, 0, 0))],
        out_specs=pl.BlockSpec((seq, hd), lambda b, h: (b, h)),
        out_shape=jax.ShapeDtypeStruct((batch * seq, heads * hd), BF16),
        compiler_params=_cparams("parallel", "parallel"),
        name="na_attention",
    )(proj, proj, proj, bias)


def _lru_kernel(x_ref, y_ref, cw_ref, cb_ref, wa_ref, ba_ref, wx_ref, bx_ref, ls_ref, o_ref,
                a0_ref, b0_ref, a1_ref, b1_ref, *, seq):
    x = x_ref[...].astype(F32)
    t_idx = lax.broadcasted_iota(I32, x.shape, 0)
    xc = (cw_ref[0:1, :] * jnp.where(t_idx >= 2, pltpu.roll(x, 2, 0), 0.0)
          + cw_ref[1:2, :] * jnp.where(t_idx >= 1, pltpu.roll(x, 1, 0), 0.0)
          + cw_ref[2:3, :] * x
          + cw_ref[3:4, :] * jnp.where(t_idx < seq - 1, pltpu.roll(x, seq - 1, 0), 0.0)
          + cb_ref[...])
    xb = xc.astype(BF16)
    for d, (a_ref, b_ref) in enumerate(((a0_ref, b0_ref), (a1_ref, b1_ref))):
        r = jax.nn.sigmoid(jnp.dot(xb, wa_ref[d], preferred_element_type=F32) + ba_ref[d:d + 1, :])
        ig = jax.nn.sigmoid(jnp.dot(xb, wx_ref[d], preferred_element_type=F32) + bx_ref[d:d + 1, :])
        a = jnp.exp(ls_ref[d:d + 1, :] * r)
        a_ref[...] = a
        b_ref[...] = jnp.sqrt(1.0 - a * a) * (ig * xc)

    n_tiles = seq // SUBLANES
    row = lax.broadcasted_iota(I32, (SUBLANES, LANES), 0)

    def tile_scan(a, b, carry, reverse):
        for s in (1, 2, 4):
            if reverse:
                keep = row < SUBLANES - s
                shift = SUBLANES - s
            else:
                keep = row >= s
                shift = s
            a_sh = jnp.where(keep, pltpu.roll(a, shift, 0), 1.0)
            b_sh = jnp.where(keep, pltpu.roll(b, shift, 0), 0.0)
            b = a * b_sh + b
            a = a * a_sh
        e = 0 if reverse else SUBLANES - 1
        a_edge = jnp.broadcast_to(a[e:e + 1, :], (SUBLANES, LANES))
        b_edge = jnp.broadcast_to(b[e:e + 1, :], (SUBLANES, LANES))
        return b + a * carry, b_edge + a_edge * carry

    def scan_body(v, carry):
        cf, cb = carry
        rf = pl.multiple_of(v * SUBLANES, SUBLANES)
        rb = pl.multiple_of((n_tiles - 1 - v) * SUBLANES, SUBLANES)
        hf, cf = tile_scan(a0_ref[pl.ds(rf, SUBLANES), :], b0_ref[pl.ds(rf, SUBLANES), :], cf, False)
        hb, cb = tile_scan(a1_ref[pl.ds(rb, SUBLANES), :], b1_ref[pl.ds(rb, SUBLANES), :], cb, True)
        b0_ref[pl.ds(rf, SUBLANES), :] = hf
        b1_ref[pl.ds(rb, SUBLANES), :] = hb
        return cf, cb

    zeros = jnp.zeros((SUBLANES, LANES), F32)
    lax.fori_loop(0, n_tiles, scan_body, (zeros, zeros), unroll=4)

    h = b0_ref[...] + b1_ref[...]
    y = y_ref[...].astype(F32)
    o_ref[...] = (h * jax.nn.gelu(y, approximate=True)).astype(o_ref.dtype)


def _rglru(proj, conv_w, conv_b, wa, ba, wx, bx, log_scale, *, row0, batch, seq, col_x, col_y):
    nb = wa.shape[1]
    bw = wa.shape[2]
    assert bw == LANES and row0 % seq == 0 and seq % (4 * SUBLANES) == 0
    rb = row0 // seq
    cvec = lambda rows_: pl.BlockSpec((rows_, bw), lambda b, n: (0, n))
    wspec = pl.BlockSpec((2, None, bw, bw), lambda b, n: (0, n, 0, 0))
    return pl.pallas_call(
        functools.partial(_lru_kernel, seq=seq),
        grid=(batch, nb),
        in_specs=[pl.BlockSpec((seq, bw), lambda b, n: (rb + b, col_x // bw + n)),
                  pl.BlockSpec((seq, bw), lambda b, n: (rb + b, col_y // bw + n)),
                  cvec(conv_w.shape[0]), cvec(1), wspec, cvec(2), wspec, cvec(2), cvec(2)],
        out_specs=pl.BlockSpec((seq, bw), lambda b, n: (b, n)),
        out_shape=jax.ShapeDtypeStruct((batch * seq, nb * bw), BF16),
        scratch_shapes=[pltpu.VMEM((seq, bw), F32)] * 4,
        compiler_params=_cparams("parallel", "parallel"),
        name="rglru",
    )(proj, proj, conv_w, conv_b, wa, ba, wx, bx, log_scale)


def _with_ones_column(v):
    lane = lax.broadcasted_iota(I32, v.shape, 1)
    ones_col = jnp.where(lane == 0, 1.0, 0.0).astype(v.dtype)
    return jnp.concatenate([v, ones_col], axis=1)


def _diff_kernel(sc_ref, q_ref, k_ref, v_ref, g_ref, o_ref, *, tq, seq, dc, scale):
    h = pl.program_id(1)
    lam = sc_ref[0]
    out_scale = sc_ref[1]
    slope = sc_ref[2 + h]
    q = q_ref[...]
    k = k_ref[...]
    v = v_ref[...]
    lane = lax.broadcasted_iota(I32, q.shape, 1)
    qpos = pl.program_id(2) * tq + lax.broadcasted_iota(I32, (tq, 1), 0)
    kpos = lax.broadcasted_iota(I32, (1, seq), 1)
    bias = jnp.abs(qpos - kpos).astype(F32) * (-slope * LOG2E)
    v_ones = _with_ones_column(v)
    outs = []
    for m_idx in range(2):
        in_map = (lane < dc) if m_idx == 0 else (lane >= dc)
        qm = jnp.where(in_map, q, jnp.zeros_like(q))
        qm = (qm.astype(F32) * (scale * LOG2E)).astype(BF16)
        s = lax.dot_general(qm, k, (((1,), (1,)), ((), ())), preferred_element_type=F32) + bias
        mx = jnp.max(s, axis=-1, keepdims=True)
        p = jnp.exp2(s - mx)
        pv = jnp.dot(p.astype(BF16), v_ones, preferred_element_type=F32)
        outs.append(pv[:, :LANES] / pv[:, LANES:LANES + 1])
    o = outs[0] - lam * outs[1]
    o = o * lax.rsqrt(jnp.mean(o * o, axis=-1, keepdims=True) + LN_EPS)
    o_ref[...] = (o * g_ref[...] * out_scale).astype(o_ref.dtype)


def _diff_attention(proj, scalars, subln_g, *, row0, batch, seq, heads, dc, col_q, col_k, col_v, tq):
    hd = 2 * dc
    assert hd == LANES and row0 % seq == 0 and seq % tq == 0
    nq = seq // tq
    rb = row0 // seq
    return pl.pallas_call(
        functools.partial(_diff_kernel, tq=tq, seq=seq, dc=dc, scale=dc ** -0.5),
        grid=(batch, heads, nq),
        in_specs=[pl.BlockSpec(memory_space=pltpu.SMEM),
                  pl.BlockSpec((tq, hd), lambda b, h, i: ((rb + b) * nq + i, col_q // hd + h)),
                  pl.BlockSpec((seq, hd), lambda b, h, i: (rb + b, col_k // hd + h)),
                  pl.BlockSpec((seq, hd), lambda b, h, i: (rb + b, col_v // hd + h)),
                  pl.BlockSpec((1, hd), lambda b, h, i: (0, 0))],
        out_specs=pl.BlockSpec((tq, hd), lambda b, h, i: (b * nq + i, h)),
        out_shape=jax.ShapeDtypeStruct((batch * seq, heads * hd), BF16),
        compiler_params=_cparams("parallel", "parallel", "arbitrary"),
        name="diff_attention",
    )(scalars, proj, proj, proj, subln_g)


def _layer_norm(y, g, b):
    mu = jnp.mean(y, axis=-1, keepdims=True)
    yc = y - mu
    var = jnp.mean(yc * yc, axis=-1, keepdims=True)
    return yc * lax.rsqrt(var + LN_EPS) * g + b


def _packed_rows(d):
    return d // (2 * LANES)


def _store_packed(ref, y):
    m, d = y.shape
    pr = _packed_rows(d)
    lo = lax.bitcast_convert_type(y[:, :d // 2].astype(BF16).astype(F32), jnp.uint32)
    hi = lax.bitcast_convert_type(y[:, d // 2:].astype(BF16).astype(F32), jnp.uint32)
    word = hi | (lo >> 16)
    for c in range(pr):
        ref[pl.ds(c, m, stride=pr), :] = word[:, c * LANES:(c + 1) * LANES]


def _load_packed(ref, c, m, pr, row0=0):
    word = ref[pl.ds(row0 * pr + c, m, stride=pr), :]
    lo = lax.bitcast_convert_type(word << 16, F32)
    hi = lax.bitcast_convert_type(word & jnp.uint32(0xFFFF0000), F32)
    return lo, hi


def _merge_kernel(*refs, d, alpha, offsets):
    g = len(offsets)
    i = pl.program_id(0)
    x_refs, oa_refs, ob_refs, oc_refs = (refs[j * g:(j + 1) * g] for j in range(4))
    g_ref, wa_ref, wb_ref, wc_ref, wo_ref, lg_ref, lb_ref, o_ref, op_ref = refs[4 * g:]
    x = _stream_tile(x_refs, offsets, i)
    merged = None
    for idx, (o_refs, w_br) in enumerate(((oa_refs, wa_ref), (ob_refs, wb_ref), (oc_refs, wc_ref))):
        gate = jax.nn.sigmoid(g_ref[:, idx * d:(idx + 1) * d].astype(F32))
        term = gate * jnp.dot(_stream_tile(o_refs, offsets, i), w_br[...], preferred_element_type=F32)
        merged = term if merged is None else merged + term
    y = alpha * x + jnp.dot(merged.astype(BF16), wo_ref[...], preferred_element_type=F32)
    y = _layer_norm(y, lg_ref[...], lb_ref[...])
    o_ref[...] = y
    _store_packed(op_ref, y)


def _merge_out_ln(xs, proj, oas, obs, ocs, wa, wb, wc, wo, ln_g, ln_b, *, alpha, tm):
    d = xs[0].shape[1]
    pr = _packed_rows(d)
    offsets, x_specs, n_blocks = _stream_specs(xs, tm)
    branch_specs = []
    for os_ in (oas, obs, ocs):
        branch_specs += _stream_specs(os_, tm)[1]
    n = n_blocks * tm
    row = lambda width: pl.BlockSpec((tm, width), lambda i: (i, 0))
    full = lambda arr: _resident(arr.shape, lambda i: (0, 0))
    return pl.pallas_call(
        functools.partial(_merge_kernel, d=d, alpha=alpha, offsets=offsets),
        grid=(n_blocks,),
        in_specs=x_specs + branch_specs + [row(3 * d), full(wa), full(wb), full(wc), full(wo), full(ln_g),
                                           full(ln_b)],
        out_specs=[row(d), pl.BlockSpec((tm * pr, LANES), lambda i: (i, 0))],
        out_shape=[jax.ShapeDtypeStruct((n, d), F32), jax.ShapeDtypeStruct((n * pr, LANES), jnp.uint32)],
        compiler_params=_cparams("parallel"),
        name="merge_out_ln1",
    )(*xs, *oas, *obs, *ocs, proj, wa, wb, wc, wo, ln_g, ln_b)


def _router_kernel(x_ref, w_ref, b_ref, idx_ref, wgt_ref, rank_ref, cnt_ref, carry_ref, *, tt, n_exp):
    @pl.when(pl.program_id(0) == 0)
    def _():
        carry_ref[...] = jnp.zeros_like(carry_ref)

    x = x_ref[...]
    w = w_ref[...]
    x_hi = x.astype(BF16)
    x_lo = (x - x_hi.astype(F32)).astype(BF16)
    w_hi = w.astype(BF16)
    w_lo = (w - w_hi.astype(F32)).astype(BF16)
    mm = functools.partial(jnp.dot, preferred_element_type=F32)
    logits = mm(x_hi, w_hi) + (mm(x_hi, w_lo) + mm(x_lo, w_hi)) + b_ref[...]
    lane = lax.broadcasted_iota(I32, (tt, n_exp), 1).astype(F32)
    work = logits
    vals, idxs, hots = [], [], []
    for _ in range(TOP_K):
        m = jnp.max(work, axis=-1, keepdims=True)
        idx = jnp.min(jnp.where(work == m, lane, float(n_exp)), axis=-1, keepdims=True)
        hot = lane == idx
        vals.append(m)
        idxs.append(idx)
        hots.append(hot)
        work = jnp.where(hot, -jnp.inf, work)
    exps = [jnp.exp(v - vals[0]) for v in vals]
    denom = exps[0]
    for e in exps[1:]:
        denom = denom + e
    onehot = jnp.zeros((tt, n_exp), F32)
    for hot in hots:
        onehot = onehot + hot.astype(F32)
    r_i = lax.broadcasted_iota(I32, (tt, tt), 0)
    c_i = lax.broadcasted_iota(I32, (tt, tt), 1)
    lower = jnp.where(r_i > c_i, 1.0, 0.0).astype(BF16)
    before = jnp.dot(lower, onehot.astype(BF16), preferred_element_type=F32) + carry_ref[...]
    for k in range(TOP_K):
        idx_ref[:, k:k + 1] = idxs[k].astype(I32)
        wgt_ref[:, k:k + 1] = exps[k] / denom
        rank_ref[:, k:k + 1] = jnp.sum(jnp.where(hots[k], before, 0.0), axis=-1, keepdims=True).astype(I32)
    carry_ref[...] = carry_ref[...] + jnp.sum(onehot, axis=0, keepdims=True)
    cnt_ref[...] = carry_ref[...].astype(I32)


def _router(x, w_router, b_router, *, tt):
    n, d = x.shape
    n_exp = w_router.shape[1]
    out4 = lambda dt: jax.ShapeDtypeStruct((n, TOP_K), dt)
    spec4 = pl.BlockSpec((tt, TOP_K), lambda i: (i, 0))
    return pl.pallas_call(
        functools.partial(_router_kernel, tt=tt, n_exp=n_exp),
        grid=(n // tt,),
        in_specs=[pl.BlockSpec((tt, d), lambda i: (i, 0)),
                  _resident((d, n_exp), lambda i: (0, 0)),
                  _resident((1, n_exp), lambda i: (0, 0))],
        out_specs=[spec4, spec4, spec4, pl.BlockSpec((1, n_exp), lambda i: (0, 0))],
        out_shape=[out4(I32), out4(F32), out4(I32), jax.ShapeDtypeStruct((1, n_exp), I32)],
        scratch_shapes=[pltpu.VMEM((1, n_exp), F32)],
        compiler_params=_cparams("arbitrary"),
        name="moe_router",
    )(x, w_router, b_router)


def _w1_prep_kernel(w_ref, o_ref):
    grp = 2 * LANES
    r = lax.broadcasted_iota(I32, (grp, grp), 0)
    c = lax.broadcasted_iota(I32, (grp, grp), 1)
    src = jnp.where(c < LANES, 2 * c, 2 * (c - LANES) + 1)
    perm = jnp.where(r == src, 1.0, 0.0).astype(BF16)
    for g in range(w_ref.shape[1] // grp):
        blk = w_ref[:, g * grp:(g + 1) * grp].astype(BF16)
        o_ref[:, g * grp:(g + 1) * grp] = jnp.dot(blk, perm, preferred_element_type=F32).astype(BF16)


def _w1_prep(w1, layer, *, td, tc):
    _, n_exp, d, two_f = w1.shape
    return pl.pallas_call(
        _w1_prep_kernel,
        grid=(n_exp, d // td, two_f // tc),
        in_specs=[pl.BlockSpec((None, None, td, tc), lambda e, i, j: (layer, e, i, j))],
        out_specs=pl.BlockSpec((None, td, tc), lambda e, i, j: (e, i, j)),
        out_shape=jax.ShapeDtypeStruct(w1.shape[1:], BF16),
        compiler_params=_cparams("parallel", "parallel", "parallel"),
        name="w1_prep",
    )(w1)


def _expert_kernel(te_ref, tv_ref, rt0_ref, rt1_ref, rta_ref, xp_ref, w1_ref, b1g_ref, b1l_ref, w2_ref, b2_ref,
                   o_ref, xg_ref, sem_ref, xb_ref, acc_ref, *, tm, tf, d, nf):
    i = pl.program_id(0)
    f = pl.program_id(1)
    last = nf - 1
    pr = _packed_rows(d)
    n_slots = EXPERT_LOOKAHEAD + 1
    slot = i % n_slots
    live = tv_ref[i] > 0
    fetched = (i < EXPERT_LOOKAHEAD) | (tv_ref[jnp.maximum(i - EXPERT_LOOKAHEAD, 0)] > 0)

    def row_copy(tok_ref, r, slot_):
        src = xp_ref.at[pl.ds(pl.multiple_of(tok_ref[r] * pr, pr), pr)]
        dst = xg_ref.at[slot_, pl.ds(pl.multiple_of(r * pr, pr), pr)]
        return pltpu.make_async_copy(src, dst, sem_ref.at[slot_])

    @pl.when((i == 0) & (f == 0))
    def _():
        for tile, tok_ref in enumerate((rt0_ref, rt1_ref)):
            def body(r, carry, tile=tile, tok_ref=tok_ref):
                row_copy(tok_ref, r, tile).start()
                return carry
            lax.fori_loop(0, tm, body, 0, unroll=8)

    @pl.when(fetched & (f == 0))
    def _():
        pltpu.make_async_copy(xg_ref.at[slot], xg_ref.at[slot], sem_ref.at[slot]).wait()

    @pl.when(live & (f == 0))
    def _():
        for c in range(pr):
            lo, hi = _load_packed(xg_ref.at[slot], c, tm, pr)
            xb_ref[:, c * LANES:(c + 1) * LANES] = lo.astype(BF16)
            xb_ref[:, d // 2 + c * LANES:d // 2 + (c + 1) * LANES] = hi.astype(BF16)
        acc_ref[...] = jnp.zeros_like(acc_ref)

    @pl.when(live)
    def _():
        chunk = tm // nf
        ahead_slot = (i + EXPERT_LOOKAHEAD) % n_slots
        for r in range(chunk):
            row_copy(rta_ref, f * chunk + r, ahead_slot).start()
        h = jnp.dot(xb_ref[...], w1_ref[...], preferred_element_type=F32)
        acts = []
        for j in range(tf // LANES):
            hg = h[:, 2 * j * LANES:(2 * j + 1) * LANES] + b1g_ref[:, j * LANES:(j + 1) * LANES]
            hl = h[:, (2 * j + 1) * LANES:(2 * j + 2) * LANES] + b1l_ref[:, j * LANES:(j + 1) * LANES]
            glu = jnp.minimum(hg, SWIGLU_LIMIT)
            lin = jnp.clip(hl, -SWIGLU_LIMIT, SWIGLU_LIMIT)
            acts.append((glu * jax.nn.sigmoid(SWIGLU_ALPHA * glu) * (lin + 1.0)).astype(BF16))
        act = jnp.concatenate(acts, axis=1)
        acc_ref[...] += jnp.dot(act, w2_ref[...], preferred_element_type=F32)

    @pl.when(live & (f == last))
    def _():
        _store_packed(o_ref, acc_ref[...] + b2_ref[...])

    @pl.when(jnp.logical_not(live) & (f == last))
    def _():
        o_ref[...] = jnp.zeros_like(o_ref)


def _expert_mlp(xp, row_tok, tile_expert, tile_valid, w1p, b1g, b1l, w2, layer, b2, *, tm, tf):
    d = w2.shape[3]
    pr = _packed_rows(d)
    n_tiles = row_tok.shape[0] // tm
    nf = w2.shape[2] // tf
    fidx = lambda f, tv, i: jnp.where(tv[i] > 0, f, nf - 1)
    grid_spec = pltpu.PrefetchScalarGridSpec(
        num_scalar_prefetch=2,
        grid=(n_tiles, nf),
        in_specs=[pl.BlockSpec((tm,), lambda i, f, te, tv: (0,), memory_space=pltpu.SMEM),
                  pl.BlockSpec((tm,), lambda i, f, te, tv: (1,), memory_space=pltpu.SMEM),
                  pl.BlockSpec((tm,), lambda i, f, te, tv: (jnp.minimum(i + EXPERT_LOOKAHEAD, n_tiles - 1),),
                               memory_space=pltpu.SMEM),
                  pl.BlockSpec(memory_space=pl.ANY),
                  pl.BlockSpec((None, d, 2 * tf), lambda i, f, te, tv: (te[i], 0, fidx(f, tv, i))),
                  pl.BlockSpec((None, 1, tf), lambda i, f, te, tv: (te[i], 0, fidx(f, tv, i))),
                  pl.BlockSpec((None, 1, tf), lambda i, f, te, tv: (te[i], 0, fidx(f, tv, i))),
                  pl.BlockSpec((None, None, tf, d), lambda i, f, te, tv: (layer, te[i], fidx(f, tv, i), 0)),
                  pl.BlockSpec((None, 1, d), lambda i, f, te, tv: (te[i], 0, 0))],
        out_specs=pl.BlockSpec((tm * pr, LANES), lambda i, f, te, tv: (i, 0)),
        scratch_shapes=[pltpu.VMEM((EXPERT_LOOKAHEAD + 1, tm * pr, LANES), jnp.uint32),
                        pltpu.SemaphoreType.DMA((EXPERT_LOOKAHEAD + 1,)),
                        pltpu.VMEM((tm, d), BF16), pltpu.VMEM((tm, d), F32)],
    )
    return pl.pallas_call(
        functools.partial(_expert_kernel, tm=tm, tf=tf, d=d, nf=nf),
        grid_spec=grid_spec,
        out_shape=jax.ShapeDtypeStruct((n_tiles * tm * pr, LANES), jnp.uint32),
        compiler_params=_cparams("arbitrary", "arbitrary"),
        name="expert_mlp",
    )(tile_expert, tile_valid, row_tok, row_tok, row_tok, xp, w1p, b1g, b1l, w2, b2)


def _combine_kernel(dc_ref, dn_ref, x_ref, yp_ref, w_ref, lg_ref, lb_ref, o_ref, yg_ref, sem_ref, ysc_ref,
                    *, alpha, tt, d):
    i = pl.program_id(0)
    n_steps = pl.num_programs(0)
    pr = _packed_rows(d)
    slot = i % 2

    def start_rows(dest_ref, t, slot_):
        for k in range(TOP_K):
            src = yp_ref.at[pl.ds(pl.multiple_of(dest_ref[t * TOP_K + k] * pr, pr), pr)]
            dst = yg_ref.at[slot_, pl.ds(pl.multiple_of((k * tt + t) * pr, pr), pr)]
            pltpu.make_async_copy(src, dst, sem_ref.at[slot_]).start()

    def wait_slot(slot_):
        pltpu.make_async_copy(yg_ref.at[slot_], yg_ref.at[slot_], sem_ref.at[slot_]).wait()

    @pl.when(i == 0)
    def _():
        def body(t, carry):
            start_rows(dc_ref, t, 0)
            return carry
        lax.fori_loop(0, tt, body, 0, unroll=2)

    wait_slot(slot)

    tokens_per_group = tt // pr
    for c in range(pr):
        for t in range(c * tokens_per_group, (c + 1) * tokens_per_group):
            start_rows(dn_ref, t, 1 - slot)
        lo_cols = slice(c * LANES, (c + 1) * LANES)
        hi_cols = slice(d // 2 + c * LANES, d // 2 + (c + 1) * LANES)
        acc_lo = alpha * x_ref[:, lo_cols]
        acc_hi = alpha * x_ref[:, hi_cols]
        for k in range(TOP_K):
            lo, hi = _load_packed(yg_ref.at[slot], c, tt, pr, row0=k * tt)
            wk = w_ref[:, k:k + 1]
            acc_lo = acc_lo + wk * lo
            acc_hi = acc_hi + wk * hi
        ysc_ref[:, lo_cols] = acc_lo
        ysc_ref[:, hi_cols] = acc_hi
    o_ref[...] = _layer_norm(ysc_ref[...], lg_ref[...], lb_ref[...])

    @pl.when(i == n_steps - 1)
    def _():
        wait_slot(1 - slot)


def _combine_ln(x, yp, dest, top_w, ln_g, ln_b, *, alpha, tt, row0, nrows):
    d = x.shape[1]
    pr = _packed_rows(d)
    assert row0 % tt == 0 and nrows % tt == 0
    n_steps = nrows // tt
    b0 = row0 // tt
    return pl.pallas_call(
        functools.partial(_combine_kernel, alpha=alpha, tt=tt, d=d),
        grid=(n_steps,),
        in_specs=[pl.BlockSpec((tt * TOP_K,), lambda i: (b0 + i,), memory_space=pltpu.SMEM),
                  pl.BlockSpec((tt * TOP_K,), lambda i: (b0 + jnp.minimum(i + 1, n_steps - 1),),
                               memory_space=pltpu.SMEM),
                  pl.BlockSpec((tt, d), lambda i: (b0 + i, 0)),
                  pl.BlockSpec(memory_space=pl.ANY),
                  pl.BlockSpec((tt, TOP_K), lambda i: (b0 + i, 0)),
                  _resident((1, d), lambda i: (0, 0)),
                  _resident((1, d), lambda i: (0, 0))],
        out_specs=pl.BlockSpec((tt, d), lambda i: (i, 0)),
        out_shape=jax.ShapeDtypeStruct((nrows, d), F32),
        scratch_shapes=[pltpu.VMEM((2, TOP_K * tt * pr, LANES), jnp.uint32), pltpu.SemaphoreType.DMA((2,)),
                        pltpu.VMEM((tt, d), F32)],
        compiler_params=_cparams("arbitrary"),
        name="combine_ln2",
    )(dest, dest, x, yp, top_w, ln_g, ln_b)


def _moe_layout(top_i, rank, counts, *, tm):
    n, k = top_i.shape
    n_exp = counts.shape[0]
    n_tiles = (n * k) // tm + n_exp + EXPERT_LOOKAHEAD
    padded = (counts + tm - 1) // tm * tm
    pends = jnp.cumsum(padded)
    pstarts = pends - padded
    first_row = jnp.sum(jnp.where(top_i[..., None] == jnp.arange(n_exp, dtype=I32), pstarts.astype(I32), 0), -1)
    dest = first_row + rank
    tile_start = jnp.arange(n_tiles, dtype=I32) * tm
    tile_expert = jnp.clip(jnp.searchsorted(pends, tile_start, side="right"), 0, n_exp - 1).astype(I32)
    tile_valid = jnp.clip(counts[tile_expert] - (tile_start - pstarts[tile_expert]), 0, tm).astype(I32)
    tok = jnp.broadcast_to(jnp.arange(n, dtype=I32)[:, None], (n, k))
    row_tok = jnp.zeros((n_tiles * tm,), I32).at[dest.reshape(-1)].set(tok.reshape(-1))
    return dest, row_tok, tile_expert, tile_valid


def _moe(x, xp, p, *, tm, tf, tt_router):
    top_i, top_w, rank, counts = _router(x, p["w_router"], p["b_router"], tt=tt_router)
    dest, row_tok, tile_expert, tile_valid = _moe_layout(top_i, rank, counts[0], tm=tm)
    yp = _expert_mlp(xp, row_tok, tile_expert, tile_valid, p["w1p"], p["b1g"], p["b1l"], p["w2"], p["layer"],
                     p["b2"], tm=tm, tf=tf)
    return yp, dest.reshape(-1), top_w


def _prepare_params(w_in, rpb_a, conv_w, conv_b, lru_wa, lru_ba, lru_wx, lru_bx, lru_lambda, lam_q1, lam_k1,
                    lam_q2, lam_k2, subln_g, w_proj_a, w_proj_b, w_proj_c, w_out, ln1_g, ln1_b, w_router,
                    b_router, w1, b1, w2, b2, ln2_g, ln2_b, *, d_model, na_rows, diff_heads, w1_prep_tiles):
    depth = w_in.shape[0]
    gate0 = w_in.shape[2] - 3 * d_model
    w2_bf16 = w2.astype(BF16)
    layers = []
    for l in range(depth):
        lam_init = 0.8 - 0.6 * math.exp(-0.3 * l)
        lam = (jnp.exp(jnp.sum(lam_q1[l].astype(F32) * lam_k1[l].astype(F32)))
               - jnp.exp(jnp.sum(lam_q2[l].astype(F32) * lam_k2[l].astype(F32))) + lam_init)
        slopes = 2.0 ** (-8.0 * (jnp.arange(diff_heads, dtype=F32) + 1.0) / diff_heads)
        layers.append(dict(
            w_in=jnp.concatenate([w_in[l, :, gate0:], w_in[l, :, :gate0]], axis=1).astype(BF16),
            na_bias={r: _na_bias_table(rpb_a[l], r) for r in na_rows},
            conv_w=conv_w[l], conv_b=conv_b[l][None, :],
            lru_wa=lru_wa[l].astype(BF16), lru_ba=lru_ba[l], lru_wx=lru_wx[l].astype(BF16), lru_bx=lru_bx[l],
            lru_log_scale=-LRU_C * jax.nn.softplus(-lru_lambda[l].astype(F32)),
            diff_scalars=jnp.concatenate([jnp.stack([lam, jnp.asarray(1.0 - lam_init, F32)]), slopes]).astype(F32),
            subln_g=subln_g[l][None, :],
            w_proj_a=w_proj_a[l].astype(BF16), w_proj_b=w_proj_b[l].astype(BF16),
            w_proj_c=w_proj_c[l].astype(BF16), w_out=w_out[l].astype(BF16),
            ln1_g=ln1_g[l][None, :], ln1_b=ln1_b[l][None, :],
            w_router=w_router[l], b_router=b_router[l][None, :],
            w1p=_w1_prep(w1, l, td=w1_prep_tiles[0], tc=w1_prep_tiles[1]),
            b1g=b1[l, :, None, 0::2], b1l=b1[l, :, None, 1::2],
            w2=w2_bf16, layer=l, b2=b2[l][:, None, :],
            ln2_g=ln2_g[l][None, :], ln2_b=ln2_b[l][None, :],
        ))
    return layers


def _encoder_layer(xs, p, groups, *, alpha, cfg):
    d = xs[0].shape[1]
    proj = _in_proj(xs, p["w_in"], cfg["proj_tm"], cfg["proj_tn"])
    na_w = cfg["na_heads"] * LANES
    rnn_w = p["lru_ba"].shape[1]
    dq = cfg["diff_heads"] * 2 * cfg["diff_dc"]
    c_qa = 3 * d
    c_xb = c_qa + 3 * na_w
    c_qc = c_xb + 2 * rnn_w
    oa, ob, oc = [], [], []
    for row0, batch, seq in groups:
        oa.append(_na_attention(proj, p["na_bias"][_na_table_rows(seq)], row0=row0, batch=batch, seq=seq, heads=cfg["na_heads"],
                                col_q=c_qa, col_k=c_qa + na_w, col_v=c_qa + 2 * na_w))
        ob.append(_rglru(proj, p["conv_w"], p["conv_b"], p["lru_wa"], p["lru_ba"], p["lru_wx"], p["lru_bx"],
                         p["lru_log_scale"], row0=row0, batch=batch, seq=seq, col_x=c_xb, col_y=c_xb + rnn_w))
        oc.append(_diff_attention(proj, p["diff_scalars"], p["subln_g"], row0=row0, batch=batch, seq=seq,
                                  heads=cfg["diff_heads"], dc=cfg["diff_dc"], col_q=c_qc, col_k=c_qc + dq,
                                  col_v=c_qc + 2 * dq, tq=cfg["diff_tq"]))
    x1, x1p = _merge_out_ln(xs, proj, oa, ob, oc, p["w_proj_a"], p["w_proj_b"], p["w_proj_c"], p["w_out"],
                            p["ln1_g"], p["ln1_b"], alpha=alpha, tm=cfg["merge_tm"])
    yp, dest, top_w = _moe(x1, x1p, p, tm=cfg["moe_tm"], tf=cfg["moe_tf"], tt_router=cfg["router_tt"])
    return tuple(_combine_ln(x1, yp, dest, top_w, p["ln2_g"], p["ln2_b"], alpha=alpha, tt=cfg["combine_tt"],
                             row0=row0, nrows=batch * seq) for row0, batch, seq in groups)


DEFAULT_CFG = dict(proj_tm=1024, proj_tn=1024, na_heads=4, diff_heads=4, diff_dc=64, diff_tq=256, merge_tm=256,
                   moe_tm=512, moe_tf=1024, router_tt=512, combine_tt=256, w1_prep_tiles=(512, 2048))


def _trunk(xs, params, cfg):
    d = xs[0].shape[-1]
    groups, row0 = [], 0
    for x in xs:
        groups.append((row0, x.shape[0], x.shape[1]))
        row0 += x.shape[0] * x.shape[1]
    na_rows = sorted({_na_table_rows(s) for _, _, s in groups})
    layers = _prepare_params(*params, d_model=d, na_rows=na_rows, diff_heads=cfg["diff_heads"],
                             w1_prep_tiles=cfg["w1_prep_tiles"])
    alpha = (2 * len(layers)) ** 0.25
    streams = tuple(x.reshape(-1, d) for x in xs)
    for p in layers:
        streams = _encoder_layer(streams, p, tuple(groups), alpha=alpha, cfg=cfg)
    return tuple(o.reshape(xin.shape) for o, xin in zip(streams, xs))


def kernel(x_prompt, x_sample, w_in, rpb_a, conv_w, conv_b, lru_wa, lru_ba, lru_wx, lru_bx, lru_lambda, lam_q1,
           lam_k1, lam_q2, lam_k2, subln_g, w_proj_a, w_proj_b, w_proj_c, w_out, ln1_g, ln1_b, w_router,
           b_router, w1, b1, w2, b2, ln2_g, ln2_b):
    params = (w_in, rpb_a, conv_w, conv_b, lru_wa, lru_ba, lru_wx, lru_bx, lru_lambda, lam_q1, lam_k1, lam_q2,
              lam_k2, subln_g, w_proj_a, w_proj_b, w_proj_c, w_out, ln1_g, ln1_b, w_router, b_router, w1, b1,
              w2, b2, ln2_g, ln2_b)
    return _trunk((x_prompt, x_sample), params, DEFAULT_CFG)
```

```python
import functools
import math

import jax
import jax.numpy as jnp
from jax import lax
from jax.experimental import pallas as pl
from jax.experimental.pallas import tpu as pltpu

F32 = jnp.float32
BF16 = jnp.bfloat16
I32 = jnp.int32

GRID_W = 64
NA_KH_MAX = 8
NA_KW = 16
NA_GROUP = 8
EXPERT_LOOKAHEAD = 2
COMBINE_LOOKAHEAD = 2
LRU_C = 8.0
TOP_K = 4
SWIGLU_LIMIT = 7.0
SWIGLU_ALPHA = 1.702
LN_EPS = 1e-5
NEG_INF = -1e30
LOG2E = math.log2(math.e)

LANES = 128
SUBLANES = 8
VMEM_LIMIT = 56 * 1024 * 1024


def _cparams(*sem):
    return pltpu.CompilerParams(dimension_semantics=sem, vmem_limit_bytes=VMEM_LIMIT)


def _resident(block_shape, index_map):
    return pl.BlockSpec(block_shape, index_map, pipeline_mode=pl.Buffered(1))


def _stream_specs(xs, tm):
    offsets, specs, off = [], [], 0
    for x in xs:
        assert x.shape[0] % tm == 0
        nb = x.shape[0] // tm
        offsets.append(off)
        specs.append(pl.BlockSpec((tm, x.shape[1]),
                                  lambda *a, off=off, nb=nb: (jnp.clip(a[0] - off, 0, nb - 1), 0)))
        off += nb
    return tuple(offsets), specs, off


def _stream_tile(x_refs, offsets, i):
    x = x_refs[0][...]
    for ref, off in zip(x_refs[1:], offsets[1:]):
        x = jnp.where(i >= off, ref[...], x)
    return x


def _in_proj_kernel(*refs, offsets):
    x_refs, (w_ref, o_ref, xb_ref) = refs[:len(offsets)], refs[len(offsets):]

    @pl.when(pl.program_id(1) == 0)
    def _():
        xb_ref[...] = _stream_tile(x_refs, offsets, pl.program_id(0)).astype(BF16)

    o_ref[...] = jnp.dot(xb_ref[...], w_ref[...], preferred_element_type=F32).astype(o_ref.dtype)


def _in_proj(xs, w, tm, tn):
    d = xs[0].shape[1]
    width = w.shape[1]
    offsets, x_specs, n_blocks = _stream_specs(xs, tm)
    return pl.pallas_call(
        functools.partial(_in_proj_kernel, offsets=offsets),
        grid=(n_blocks, width // tn),
        in_specs=x_specs + [pl.BlockSpec((d, tn), lambda i, j: (0, j))],
        out_specs=pl.BlockSpec((tm, tn), lambda i, j: (i, j)),
        out_shape=jax.ShapeDtypeStruct((n_blocks * tm, width), BF16),
        scratch_shapes=[pltpu.VMEM((tm, d), BF16)],
        compiler_params=_cparams("parallel", "arbitrary"),
        name="in_proj",
    )(*xs, w)


def _na_bias_table(rpb, rows):
    kh, grp = NA_KH_MAX, NA_GROUP
    span = min(2 * grp, rows)
    c = jnp.arange(GRID_W)[:, None]
    kc = jnp.arange(GRID_W)[None, :]
    col_off = jnp.clip(kc - c + NA_KW - 1, 0, 2 * NA_KW - 2)
    win = jnp.clip(c - NA_KW // 2, 0, GRID_W - NA_KW)
    ok = (kc >= win) & (kc < win + NA_KW)
    onehot = (col_off[..., None] == jnp.arange(2 * NA_KW - 1)).astype(F32)
    cols = jnp.einsum("hrv,ckv->hrck", rpb.astype(F32), onehot, precision=lax.Precision.HIGHEST)
    cols = jnp.where(ok, cols, NEG_INF)
    v = jnp.arange(3)[:, None, None]
    g = jnp.arange(grp)[None, :, None]
    ru = jnp.arange(span)[None, None, :]
    u0 = jnp.where(v == 0, 0, jnp.where(v == 1, kh // 2, rows - span))
    first = jnp.clip(u0 + (kh // 2) * v + g - kh // 2, 0, rows - kh) - u0
    valid = (ru >= first) & (ru < first + kh)
    row_off = jnp.clip(ru - (kh // 2) * v - g + kh - 1, 0, 2 * kh - 2)
    tab = jnp.where(valid[None, :, :, :, None, None], cols[:, row_off], NEG_INF)
    tab = tab * LOG2E
    return tab.transpose(0, 1, 2, 4, 3, 5).reshape(rpb.shape[0], 3, grp * GRID_W, span * GRID_W)


def _na_table_rows(seq):
    return min(seq // GRID_W, 3 * NA_GROUP)


def _na_kernel(q_ref, k_ref, v_ref, bias_ref, o_ref, *, rows, scale):
    kh, grp = NA_KH_MAX, NA_GROUP
    span = min(2 * grp, rows)

    def group_body(gi, carry):
        i0 = gi * grp
        u0 = jnp.clip(i0 - kh // 2, 0, rows - span)
        qoff = pl.multiple_of(i0 * GRID_W, grp * GRID_W)
        koff = pl.multiple_of(u0 * GRID_W, (kh // 2) * GRID_W)
        q = q_ref[pl.ds(qoff, grp * GRID_W), :]
        kw = k_ref[pl.ds(koff, span * GRID_W), :]
        vw = v_ref[pl.ds(koff, span * GRID_W), :]
        s = lax.dot_general(q, kw, (((1,), (1,)), ((), ())), preferred_element_type=F32)
        s = s * (scale * LOG2E) + bias_ref[(i0 - u0) // (kh // 2)]
        m = jnp.max(s, axis=-1, keepdims=True)
        p = jnp.exp2(s - m)
        pv = jnp.dot(p.astype(BF16), _with_ones_column(vw), preferred_element_type=F32)
        o = pv[:, :LANES] / pv[:, LANES:LANES + 1]
        o_ref[pl.ds(qoff, grp * GRID_W), :] = o.astype(o_ref.dtype)
        return carry

    lax.fori_loop(0, rows // grp, group_body, 0, unroll=2)


def _na_attention(proj, bias, *, row0, batch, seq, heads, col_q, col_k, col_v):
    hd = LANES
    rows = seq // GRID_W
    rb = row0 // seq
    assert row0 % seq == 0 and rows % NA_GROUP == 0 and rows >= NA_KH_MAX
    spec = lambda col: pl.BlockSpec((seq, hd), lambda b, h: (rb + b, col // hd + h))
    return pl.pallas_call(
        functools.partial(_na_kernel, rows=rows, scale=hd ** -0.5),
        grid=(batch, heads),
        in_specs=[spec(col_q), spec(col_k), spec(col_v),
                  pl.BlockSpec((None,) + bias.shape[1:], lambda b, h: (h, 0, 0, 0))],
        out_specs=pl.BlockSpec((seq, hd), lambda b, h: (b, h)),
        out_shape=jax.ShapeDtypeStruct((batch * seq, heads * hd), BF16),
        compiler_params=_cparams("parallel", "parallel"),
        name="na_attention",
    )(proj, proj, proj, bias)


def _lru_kernel(x_ref, y_ref, cw_ref, cb_ref, wa_ref, ba_ref, wx_ref, bx_ref, ls_ref, o_ref,
                a0_ref, b0_ref, a1_ref, b1_ref, *, seq):
    x = x_ref[...].astype(F32)
    t_idx = lax.broadcasted_iota(I32, x.shape, 0)
    xc = (cw_ref[0:1, :] * jnp.where(t_idx >= 2, pltpu.roll(x, 2, 0), 0.0)
          + cw_ref[1:2, :] * jnp.where(t_idx >= 1, pltpu.roll(x, 1, 0), 0.0)
          + cw_ref[2:3, :] * x
          + cw_ref[3:4, :] * jnp.where(t_idx < seq - 1, pltpu.roll(x, seq - 1, 0), 0.0)
          + cb_ref[...])
    xb = xc.astype(BF16)
    n_tiles = seq // SUBLANES
    row3 = lax.broadcasted_iota(I32, (n_tiles, SUBLANES, LANES), 1)

    def tile_prefix(a, b, reverse):
        a = a.reshape(n_tiles, SUBLANES, LANES)
        b = b.reshape(n_tiles, SUBLANES, LANES)
        for s in (1, 2, 4):
            if reverse:
                keep = row3 < SUBLANES - s
                shift = SUBLANES - s
            else:
                keep = row3 >= s
                shift = s
            a_sh = jnp.where(keep, pltpu.roll(a, shift, 1), 1.0)
            b_sh = jnp.where(keep, pltpu.roll(b, shift, 1), 0.0)
            b = a * b_sh + b
            a = a * a_sh
        return a.reshape(seq, LANES), b.reshape(seq, LANES)

    for d, (a_ref, b_ref) in enumerate(((a0_ref, b0_ref), (a1_ref, b1_ref))):
        r = jax.nn.sigmoid(jnp.dot(xb, wa_ref[d], preferred_element_type=F32) + ba_ref[d:d + 1, :])
        ig = jax.nn.sigmoid(jnp.dot(xb, wx_ref[d], preferred_element_type=F32) + bx_ref[d:d + 1, :])
        a = jnp.exp(ls_ref[d:d + 1, :] * r)
        b = jnp.sqrt(1.0 - a * a) * (ig * xc)
        a_ref[...], b_ref[...] = tile_prefix(a, b, reverse=(d == 1))

    def tile_apply(a, b, carry, reverse):
        e = 0 if reverse else SUBLANES - 1
        a_edge = jnp.broadcast_to(a[e:e + 1, :], (SUBLANES, LANES))
        b_edge = jnp.broadcast_to(b[e:e + 1, :], (SUBLANES, LANES))
        return b + a * carry, b_edge + a_edge * carry

    def scan_body(v, carry):
        cf, cb = carry
        rf = pl.multiple_of(v * SUBLANES, SUBLANES)
        rb = pl.multiple_of((n_tiles - 1 - v) * SUBLANES, SUBLANES)
        hf, cf = tile_apply(a0_ref[pl.ds(rf, SUBLANES), :], b0_ref[pl.ds(rf, SUBLANES), :], cf, False)
        hb, cb = tile_apply(a1_ref[pl.ds(rb, SUBLANES), :], b1_ref[pl.ds(rb, SUBLANES), :], cb, True)
        b0_ref[pl.ds(rf, SUBLANES), :] = hf
        b1_ref[pl.ds(rb, SUBLANES), :] = hb
        return cf, cb

    zeros = jnp.zeros((SUBLANES, LANES), F32)
    lax.fori_loop(0, n_tiles, scan_body, (zeros, zeros), unroll=8)

    h = b0_ref[...] + b1_ref[...]
    y = y_ref[...].astype(F32)
    o_ref[...] = (h * jax.nn.gelu(y, approximate=True)).astype(o_ref.dtype)


def _rglru(proj, conv_w, conv_b, wa, ba, wx, bx, log_scale, *, row0, batch, seq, col_x, col_y):
    nb = wa.shape[1]
    bw = wa.shape[2]
    assert bw == LANES and row0 % seq == 0 and seq % (8 * SUBLANES) == 0
    rb = row0 // seq
    cvec = lambda rows_: pl.BlockSpec((rows_, bw), lambda b, n: (0, n))
    wspec = pl.BlockSpec((2, None, bw, bw), lambda b, n: (0, n, 0, 0))
    return pl.pallas_call(
        functools.partial(_lru_kernel, seq=seq),
        grid=(batch, nb),
        in_specs=[pl.BlockSpec((seq, bw), lambda b, n: (rb + b, col_x // bw + n)),
                  pl.BlockSpec((seq, bw), lambda b, n: (rb + b, col_y // bw + n)),
                  cvec(conv_w.shape[0]), cvec(1), wspec, cvec(2), wspec, cvec(2), cvec(2)],
        out_specs=pl.BlockSpec((seq, bw), lambda b, n: (b, n)),
        out_shape=jax.ShapeDtypeStruct((batch * seq, nb * bw), BF16),
        scratch_shapes=[pltpu.VMEM((seq, bw), F32)] * 4,
        compiler_params=_cparams("parallel", "parallel"),
        name="rglru",
    )(proj, proj, conv_w, conv_b, wa, ba, wx, bx, log_scale)


def _with_ones_column(v):
    lane = lax.broadcasted_iota(I32, v.shape, 1)
    ones_col = jnp.where(lane == 0, 1.0, 0.0).astype(v.dtype)
    return jnp.concatenate([v, ones_col], axis=1)


def _diff_kernel(sc_ref, q_ref, k_ref, v_ref, g_ref, o_ref, *, tq, seq, dc, scale):
    h = pl.program_id(1)
    lam = sc_ref[0]
    out_scale = sc_ref[1]
    slope = sc_ref[2 + h]
    q = q_ref[...]
    k = k_ref[...]
    v = v_ref[...]
    lane = lax.broadcasted_iota(I32, q.shape, 1)
    qpos = pl.program_id(2) * tq + lax.broadcasted_iota(I32, (tq, 1), 0)
    kpos = lax.broadcasted_iota(I32, (1, seq), 1)
    bias = jnp.abs(qpos - kpos).astype(F32) * (-slope * LOG2E)
    v_ones = _with_ones_column(v)
    outs = []
    for m_idx in range(2):
        in_map = (lane < dc) if m_idx == 0 else (lane >= dc)
        qm = jnp.where(in_map, q, jnp.zeros_like(q))
        qm = (qm.astype(F32) * (scale * LOG2E)).astype(BF16)
        s = lax.dot_general(qm, k, (((1,), (1,)), ((), ())), preferred_element_type=F32) + bias
        mx = jnp.max(s, axis=-1, keepdims=True)
        p = jnp.exp2(s - mx)
        pv = jnp.dot(p.astype(BF16), v_ones, preferred_element_type=F32)
        outs.append(pv[:, :LANES] / pv[:, LANES:LANES + 1])
    o = outs[0] - lam * outs[1]
    o = o * lax.rsqrt(jnp.mean(o * o, axis=-1, keepdims=True) + LN_EPS)
    o_ref[...] = (o * g_ref[...] * out_scale).astype(o_ref.dtype)


def _diff_attention(proj, scalars, subln_g, *, row0, batch, seq, heads, dc, col_q, col_k, col_v, tq):
    hd = 2 * dc
    assert hd == LANES and row0 % seq == 0 and seq % tq == 0
    nq = seq // tq
    rb = row0 // seq
    return pl.pallas_call(
        functools.partial(_diff_kernel, tq=tq, seq=seq, dc=dc, scale=dc ** -0.5),
        grid=(batch, heads, nq),
        in_specs=[pl.BlockSpec(memory_space=pltpu.SMEM),
                  pl.BlockSpec((tq, hd), lambda b, h, i: ((rb + b) * nq + i, col_q // hd + h)),
                  pl.BlockSpec((seq, hd), lambda b, h, i: (rb + b, col_k // hd + h)),
                  pl.BlockSpec((seq, hd), lambda b, h, i: (rb + b, col_v // hd + h)),
                  pl.BlockSpec((1, hd), lambda b, h, i: (0, 0))],
        out_specs=pl.BlockSpec((tq, hd), lambda b, h, i: (b * nq + i, h)),
        out_shape=jax.ShapeDtypeStruct((batch * seq, heads * hd), BF16),
        compiler_params=_cparams("parallel", "parallel", "arbitrary"),
        name="diff_attention",
    )(scalars, proj, proj, proj, subln_g)


def _layer_norm(y, g, b):
    mu = jnp.mean(y, axis=-1, keepdims=True)
    yc = y - mu
    var = jnp.mean(yc * yc, axis=-1, keepdims=True)
    return yc * lax.rsqrt(var + LN_EPS) * g + b


def _packed_rows(d):
    return d // (2 * LANES)


def _store_packed(ref, y):
    m, d = y.shape
    pr = _packed_rows(d)
    lo = lax.bitcast_convert_type(y[:, :d // 2].astype(BF16).astype(F32), jnp.uint32)
    hi = lax.bitcast_convert_type(y[:, d // 2:].astype(BF16).astype(F32), jnp.uint32)
    word = hi | (lo >> 16)
    for c in range(pr):
        ref[pl.ds(c, m, stride=pr), :] = word[:, c * LANES:(c + 1) * LANES]


def _load_packed(ref, c, m, pr, row0=0):
    word = ref[pl.ds(row0 * pr + c, m, stride=pr), :]
    lo = lax.bitcast_convert_type(word << 16, F32)
    hi = lax.bitcast_convert_type(word & jnp.uint32(0xFFFF0000), F32)
    return lo, hi


def _merge_kernel(*refs, d, alpha, offsets):
    g = len(offsets)
    i = pl.program_id(0)
    x_refs, oa_refs, ob_refs, oc_refs = (refs[j * g:(j + 1) * g] for j in range(4))
    g_ref, wa_ref, wb_ref, wc_ref, wo_ref, lg_ref, lb_ref, o_ref, op_ref = refs[4 * g:]
    x = _stream_tile(x_refs, offsets, i)
    merged = None
    for idx, (o_refs, w_br) in enumerate(((oa_refs, wa_ref), (ob_refs, wb_ref), (oc_refs, wc_ref))):
        gate = jax.nn.sigmoid(g_ref[:, idx * d:(idx + 1) * d].astype(F32))
        term = gate * jnp.dot(_stream_tile(o_refs, offsets, i), w_br[...], preferred_element_type=F32)
        merged = term if merged is None else merged + term
    y = alpha * x + jnp.dot(merged.astype(BF16), wo_ref[...], preferred_element_type=F32)
    y = _layer_norm(y, lg_ref[...], lb_ref[...])
    o_ref[...] = y
    _store_packed(op_ref, y)


def _merge_out_ln(xs, proj, oas, obs, ocs, wa, wb, wc, wo, ln_g, ln_b, *, alpha, tm):
    d = xs[0].shape[1]
    pr = _packed_rows(d)
    offsets, x_specs, n_blocks = _stream_specs(xs, tm)
    branch_specs = []
    for os_ in (oas, obs, ocs):
        branch_specs += _stream_specs(os_, tm)[1]
    n = n_blocks * tm
    row = lambda width: pl.BlockSpec((tm, width), lambda i: (i, 0))
    full = lambda arr: _resident(arr.shape, lambda i: (0, 0))
    return pl.pallas_call(
        functools.partial(_merge_kernel, d=d, alpha=alpha, offsets=offsets),
        grid=(n_blocks,),
        in_specs=x_specs + branch_specs + [row(3 * d), full(wa), full(wb), full(wc), full(wo), full(ln_g),
                                           full(ln_b)],
        out_specs=[row(d), pl.BlockSpec((tm * pr, LANES), lambda i: (i, 0))],
        out_shape=[jax.ShapeDtypeStruct((n, d), F32), jax.ShapeDtypeStruct((n * pr, LANES), jnp.uint32)],
        compiler_params=_cparams("parallel"),
        name="merge_out_ln1",
    )(*xs, *oas, *obs, *ocs, proj, wa, wb, wc, wo, ln_g, ln_b)


def _router_kernel(x_ref, w_ref, b_ref, idx_ref, wgt_ref, rank_ref, cnt_ref, carry_ref, *, tt, n_exp):
    @pl.when(pl.program_id(0) == 0)
    def _():
        carry_ref[...] = jnp.zeros_like(carry_ref)

    x = x_ref[...]
    w = w_ref[...]
    x_hi = x.astype(BF16)
    x_lo = (x - x_hi.astype(F32)).astype(BF16)
    w_hi = w.astype(BF16)
    w_lo = (w - w_hi.astype(F32)).astype(BF16)
    mm = functools.partial(jnp.dot, preferred_element_type=F32)
    logits = mm(x_hi, w_hi) + (mm(x_hi, w_lo) + mm(x_lo, w_hi)) + b_ref[...]
    lane = lax.broadcasted_iota(I32, (tt, n_exp), 1).astype(F32)
    work = logits
    vals, idxs, hots = [], [], []
    for _ in range(TOP_K):
        m = jnp.max(work, axis=-1, keepdims=True)
        idx = jnp.min(jnp.where(work == m, lane, float(n_exp)), axis=-1, keepdims=True)
        hot = lane == idx
        vals.append(m)
        idxs.append(idx)
        hots.append(hot)
        work = jnp.where(hot, -jnp.inf, work)
    exps = [jnp.exp(v - vals[0]) for v in vals]
    denom = exps[0]
    for e in exps[1:]:
        denom = denom + e
    onehot = jnp.zeros((tt, n_exp), F32)
    for hot in hots:
        onehot = onehot + hot.astype(F32)
    r_i = lax.broadcasted_iota(I32, (tt, tt), 0)
    c_i = lax.broadcasted_iota(I32, (tt, tt), 1)
    lower = jnp.where(r_i > c_i, 1.0, 0.0).astype(BF16)
    before = jnp.dot(lower, onehot.astype(BF16), preferred_element_type=F32) + carry_ref[...]
    for k in range(TOP_K):
        idx_ref[:, k:k + 1] = idxs[k].astype(I32)
        wgt_ref[:, k:k + 1] = exps[k] / denom
        rank_ref[:, k:k + 1] = jnp.sum(jnp.where(hots[k], before, 0.0), axis=-1, keepdims=True).astype(I32)
    carry_ref[...] = carry_ref[...] + jnp.sum(onehot, axis=0, keepdims=True)
    cnt_ref[...] = carry_ref[...].astype(I32)


def _router(x, w_router, b_router, *, tt):
    n, d = x.shape
    n_exp = w_router.shape[1]
    out4 = lambda dt: jax.ShapeDtypeStruct((n, TOP_K), dt)
    spec4 = pl.BlockSpec((tt, TOP_K), lambda i: (i, 0))
    return pl.pallas_call(
        functools.partial(_router_kernel, tt=tt, n_exp=n_exp),
        grid=(n // tt,),
        in_specs=[pl.BlockSpec((tt, d), lambda i: (i, 0)),
                  _resident((d, n_exp), lambda i: (0, 0)),
                  _resident((1, n_exp), lambda i: (0, 0))],
        out_specs=[spec4, spec4, spec4, pl.BlockSpec((1, n_exp), lambda i: (0, 0))],
        out_shape=[out4(I32), out4(F32), out4(I32), jax.ShapeDtypeStruct((1, n_exp), I32)],
        scratch_shapes=[pltpu.VMEM((1, n_exp), F32)],
        compiler_params=_cparams("arbitrary"),
        name="moe_router",
    )(x, w_router, b_router)


def _w1_prep_kernel(w_ref, o_ref):
    grp = 2 * LANES
    r = lax.broadcasted_iota(I32, (grp, grp), 0)
    c = lax.broadcasted_iota(I32, (grp, grp), 1)
    src = jnp.where(c < LANES, 2 * c, 2 * (c - LANES) + 1)
    perm = jnp.where(r == src, 1.0, 0.0).astype(BF16)
    for g in range(w_ref.shape[1] // grp):
        blk = w_ref[:, g * grp:(g + 1) * grp].astype(BF16)
        o_ref[:, g * grp:(g + 1) * grp] = jnp.dot(blk, perm, preferred_element_type=F32).astype(BF16)


def _w1_prep(w1, layer, *, td, tc):
    _, n_exp, d, two_f = w1.shape
    return pl.pallas_call(
        _w1_prep_kernel,
        grid=(n_exp, d // td, two_f // tc),
        in_specs=[pl.BlockSpec((None, None, td, tc), lambda e, i, j: (layer, e, i, j))],
        out_specs=pl.BlockSpec((None, td, tc), lambda e, i, j: (e, i, j)),
        out_shape=jax.ShapeDtypeStruct(w1.shape[1:], BF16),
        compiler_params=_cparams("parallel", "parallel", "parallel"),
        name="w1_prep",
    )(w1)


def _expert_kernel(te_ref, tv_ref, rt0_ref, rt1_ref, rta_ref, xp_ref, w1_ref, b1g_ref, b1l_ref, w2_ref, b2_ref,
                   o_ref, xg_ref, sem_ref, xb_ref, acc_ref, *, tm, tf, d, nf):
    i = pl.program_id(0)
    f = pl.program_id(1)
    last = nf - 1
    pr = _packed_rows(d)
    n_slots = EXPERT_LOOKAHEAD + 1
    slot = i % n_slots
    live = tv_ref[i] > 0
    fetched = (i < EXPERT_LOOKAHEAD) | (tv_ref[jnp.maximum(i - EXPERT_LOOKAHEAD, 0)] > 0)

    def row_copy(tok_ref, r, slot_):
        src = xp_ref.at[pl.ds(pl.multiple_of(tok_ref[r] * pr, pr), pr)]
        dst = xg_ref.at[slot_, pl.ds(pl.multiple_of(r * pr, pr), pr)]
        return pltpu.make_async_copy(src, dst, sem_ref.at[slot_])

    @pl.when((i == 0) & (f == 0))
    def _():
        for tile, tok_ref in enumerate((rt0_ref, rt1_ref)):
            def body(r, carry, tile=tile, tok_ref=tok_ref):
                row_copy(tok_ref, r, tile).start()
                return carry
            lax.fori_loop(0, tm, body, 0, unroll=8)

    @pl.when(fetched & (f == 0))
    def _():
        pltpu.make_async_copy(xg_ref.at[slot], xg_ref.at[slot], sem_ref.at[slot]).wait()

    @pl.when(live & (f == 0))
    def _():
        for c in range(pr):
            lo, hi = _load_packed(xg_ref.at[slot], c, tm, pr)
            xb_ref[:, c * LANES:(c + 1) * LANES] = lo.astype(BF16)
            xb_ref[:, d // 2 + c * LANES:d // 2 + (c + 1) * LANES] = hi.astype(BF16)
        acc_ref[...] = jnp.zeros_like(acc_ref)

    @pl.when(live)
    def _():
        chunk = tm // nf
        ahead_slot = (i + EXPERT_LOOKAHEAD) % n_slots
        for r in range(chunk):
            row_copy(rta_ref, f * chunk + r, ahead_slot).start()
        h = jnp.dot(xb_ref[...], w1_ref[...], preferred_element_type=F32)
        acts = []
        for j in range(tf // LANES):
            hg = h[:, 2 * j * LANES:(2 * j + 1) * LANES] + b1g_ref[:, j * LANES:(j + 1) * LANES]
            hl = h[:, (2 * j + 1) * LANES:(2 * j + 2) * LANES] + b1l_ref[:, j * LANES:(j + 1) * LANES]
            glu = jnp.minimum(hg, SWIGLU_LIMIT)
            lin = jnp.clip(hl, -SWIGLU_LIMIT, SWIGLU_LIMIT)
            acts.append((glu * jax.nn.sigmoid(SWIGLU_ALPHA * glu) * (lin + 1.0)).astype(BF16))
        act = jnp.concatenate(acts, axis=1)
        acc_ref[...] += jnp.dot(act, w2_ref[...], preferred_element_type=F32)

    @pl.when(live & (f == last))
    def _():
        _store_packed(o_ref, acc_ref[...] + b2_ref[...])

    @pl.when(jnp.logical_not(live) & (f == last))
    def _():
        o_ref[...] = jnp.zeros_like(o_ref)


def _expert_mlp(xp, row_tok, tile_expert, tile_valid, w1p, b1g, b1l, w2, layer, b2, *, tm, tf):
    d = w2.shape[3]
    pr = _packed_rows(d)
    n_tiles = row_tok.shape[0] // tm
    nf = w2.shape[2] // tf
    fidx = lambda f, tv, i: jnp.where(tv[i] > 0, f, nf - 1)
    grid_spec = pltpu.PrefetchScalarGridSpec(
        num_scalar_prefetch=2,
        grid=(n_tiles, nf),
        in_specs=[pl.BlockSpec((tm,), lambda i, f, te, tv: (0,), memory_space=pltpu.SMEM),
                  pl.BlockSpec((tm,), lambda i, f, te, tv: (1,), memory_space=pltpu.SMEM),
                  pl.BlockSpec((tm,), lambda i, f, te, tv: (jnp.minimum(i + EXPERT_LOOKAHEAD, n_tiles - 1),),
                               memory_space=pltpu.SMEM),
                  pl.BlockSpec(memory_space=pl.ANY),
                  pl.BlockSpec((None, d, 2 * tf), lambda i, f, te, tv: (te[i], 0, fidx(f, tv, i))),
                  pl.BlockSpec((None, 1, tf), lambda i, f, te, tv: (te[i], 0, fidx(f, tv, i))),
                  pl.BlockSpec((None, 1, tf), lambda i, f, te, tv: (te[i], 0, fidx(f, tv, i))),
                  pl.BlockSpec((None, None, tf, d), lambda i, f, te, tv: (layer, te[i], fidx(f, tv, i), 0)),
                  pl.BlockSpec((None, 1, d), lambda i, f, te, tv: (te[i], 0, 0))],
        out_specs=pl.BlockSpec((tm * pr, LANES), lambda i, f, te, tv: (i, 0)),
        scratch_shapes=[pltpu.VMEM((EXPERT_LOOKAHEAD + 1, tm * pr, LANES), jnp.uint32),
                        pltpu.SemaphoreType.DMA((EXPERT_LOOKAHEAD + 1,)),
                        pltpu.VMEM((tm, d), BF16), pltpu.VMEM((tm, d), F32)],
    )
    return pl.pallas_call(
        functools.partial(_expert_kernel, tm=tm, tf=tf, d=d, nf=nf),
        grid_spec=grid_spec,
        out_shape=jax.ShapeDtypeStruct((n_tiles * tm * pr, LANES), jnp.uint32),
        compiler_params=_cparams("arbitrary", "arbitrary"),
        name="expert_mlp",
    )(tile_expert, tile_valid, row_tok, row_tok, row_tok, xp, w1p, b1g, b1l, w2, b2)


def _combine_kernel(d0_ref, d1_ref, da_ref, x_ref, yp_ref, w_ref, lg_ref, lb_ref, o_ref, yg_ref, sem_ref,
                    ysc_ref, *, alpha, tt, d):
    i = pl.program_id(0)
    n_steps = pl.num_programs(0)
    pr = _packed_rows(d)
    n_slots = COMBINE_LOOKAHEAD + 1
    slot = i % n_slots

    def start_rows(dest_ref, t, slot_):
        for k in range(TOP_K):
            src = yp_ref.at[pl.ds(pl.multiple_of(dest_ref[t * TOP_K + k] * pr, pr), pr)]
            dst = yg_ref.at[slot_, pl.ds(pl.multiple_of((k * tt + t) * pr, pr), pr)]
            pltpu.make_async_copy(src, dst, sem_ref.at[slot_]).start()

    def wait_slot(slot_):
        pltpu.make_async_copy(yg_ref.at[slot_], yg_ref.at[slot_], sem_ref.at[slot_]).wait()

    @pl.when(i == 0)
    def _():
        for tile, dest_ref in enumerate((d0_ref, d1_ref)):
            def body(t, carry, tile=tile, dest_ref=dest_ref):
                start_rows(dest_ref, t, tile)
                return carry
            lax.fori_loop(0, tt, body, 0, unroll=2)

    wait_slot(slot)

    ahead_slot = (i + COMBINE_LOOKAHEAD) % n_slots
    tokens_per_group = tt // pr
    for c in range(pr):
        for t in range(c * tokens_per_group, (c + 1) * tokens_per_group):
            start_rows(da_ref, t, ahead_slot)
        lo_cols = slice(c * LANES, (c + 1) * LANES)
        hi_cols = slice(d // 2 + c * LANES, d // 2 + (c + 1) * LANES)
        acc_lo = alpha * x_ref[:, lo_cols]
        acc_hi = alpha * x_ref[:, hi_cols]
        for k in range(TOP_K):
            lo, hi = _load_packed(yg_ref.at[slot], c, tt, pr, row0=k * tt)
            wk = w_ref[:, k:k + 1]
            acc_lo = acc_lo + wk * lo
            acc_hi = acc_hi + wk * hi
        ysc_ref[:, lo_cols] = acc_lo
        ysc_ref[:, hi_cols] = acc_hi
    o_ref[...] = _layer_norm(ysc_ref[...], lg_ref[...], lb_ref[...])

    @pl.when(i == n_steps - 1)
    def _():
        for ahead in range(1, COMBINE_LOOKAHEAD + 1):
            wait_slot((i + ahead) % n_slots)


def _combine_ln(x, yp, dest, top_w, ln_g, ln_b, *, alpha, tt, row0, nrows):
    d = x.shape[1]
    pr = _packed_rows(d)
    assert row0 % tt == 0 and nrows % tt == 0
    n_steps = nrows // tt
    b0 = row0 // tt
    return pl.pallas_call(
        functools.partial(_combine_kernel, alpha=alpha, tt=tt, d=d),
        grid=(n_steps,),
        in_specs=[pl.BlockSpec((tt * TOP_K,), lambda i: (b0,), memory_space=pltpu.SMEM),
                  pl.BlockSpec((tt * TOP_K,), lambda i: (b0 + min(1, n_steps - 1),), memory_space=pltpu.SMEM),
                  pl.BlockSpec((tt * TOP_K,), lambda i: (b0 + jnp.minimum(i + COMBINE_LOOKAHEAD, n_steps - 1),),
                               memory_space=pltpu.SMEM),
                  pl.BlockSpec((tt, d), lambda i: (b0 + i, 0)),
                  pl.BlockSpec(memory_space=pl.ANY),
                  pl.BlockSpec((tt, TOP_K), lambda i: (b0 + i, 0)),
                  _resident((1, d), lambda i: (0, 0)),
                  _resident((1, d), lambda i: (0, 0))],
        out_specs=pl.BlockSpec((tt, d), lambda i: (i, 0)),
        out_shape=jax.ShapeDtypeStruct((nrows, d), F32),
        scratch_shapes=[pltpu.VMEM((COMBINE_LOOKAHEAD + 1, TOP_K * tt * pr, LANES), jnp.uint32),
                        pltpu.SemaphoreType.DMA((COMBINE_LOOKAHEAD + 1,)), pltpu.VMEM((tt, d), F32)],
        compiler_params=_cparams("arbitrary"),
        name="combine_ln2",
    )(dest, dest, dest, x, yp, top_w, ln_g, ln_b)


def _moe_layout(top_i, rank, counts, *, tm):
    n, k = top_i.shape
    n_exp = counts.shape[0]
    n_tiles = (n * k) // tm + n_exp + EXPERT_LOOKAHEAD
    padded = (counts + tm - 1) // tm * tm
    pends = jnp.cumsum(padded)
    pstarts = pends - padded
    first_row = jnp.sum(jnp.where(top_i[..., None] == jnp.arange(n_exp, dtype=I32), pstarts.astype(I32), 0), -1)
    dest = first_row + rank
    tile_start = jnp.arange(n_tiles, dtype=I32) * tm
    tile_expert = jnp.sum(tile_start[:, None] >= pends[None, :], axis=1)
    tile_expert = jnp.clip(tile_expert, 0, n_exp - 1).astype(I32)
    tile_valid = jnp.clip(counts[tile_expert] - (tile_start - pstarts[tile_expert]), 0, tm).astype(I32)
    tok = jnp.broadcast_to(jnp.arange(n, dtype=I32)[:, None], (n, k))
    row_tok = jnp.zeros((n_tiles * tm,), I32).at[dest.reshape(-1)].set(tok.reshape(-1))
    return dest, row_tok, tile_expert, tile_valid


def _moe(x, xp, p, *, tm, tf, tt_router):
    top_i, top_w, rank, counts = _router(x, p["w_router"], p["b_router"], tt=tt_router)
    dest, row_tok, tile_expert, tile_valid = _moe_layout(top_i, rank, counts[0], tm=tm)
    yp = _expert_mlp(xp, row_tok, tile_expert, tile_valid, p["w1p"], p["b1g"], p["b1l"], p["w2"], p["layer"],
                     p["b2"], tm=tm, tf=tf)
    return yp, dest.reshape(-1), top_w


def _prepare_params(w_in, rpb_a, conv_w, conv_b, lru_wa, lru_ba, lru_wx, lru_bx, lru_lambda, lam_q1, lam_k1,
                    lam_q2, lam_k2, subln_g, w_proj_a, w_proj_b, w_proj_c, w_out, ln1_g, ln1_b, w_router,
                    b_router, w1, b1, w2, b2, ln2_g, ln2_b, *, d_model, na_rows, diff_heads, w1_prep_tiles):
    depth = w_in.shape[0]
    gate0 = w_in.shape[2] - 3 * d_model
    w2_bf16 = w2.astype(BF16)
    layers = []
    for l in range(depth):
        lam_init = 0.8 - 0.6 * math.exp(-0.3 * l)
        lam = (jnp.exp(jnp.sum(lam_q1[l].astype(F32) * lam_k1[l].astype(F32)))
               - jnp.exp(jnp.sum(lam_q2[l].astype(F32) * lam_k2[l].astype(F32))) + lam_init)
        slopes = 2.0 ** (-8.0 * (jnp.arange(diff_heads, dtype=F32) + 1.0) / diff_heads)
        layers.append(dict(
            w_in=jnp.concatenate([w_in[l, :, gate0:], w_in[l, :, :gate0]], axis=1).astype(BF16),
            na_bias={r: _na_bias_table(rpb_a[l], r) for r in na_rows},
            conv_w=conv_w[l], conv_b=conv_b[l][None, :],
            lru_wa=lru_wa[l].astype(BF16), lru_ba=lru_ba[l], lru_wx=lru_wx[l].astype(BF16), lru_bx=lru_bx[l],
            lru_log_scale=-LRU_C * jax.nn.softplus(-lru_lambda[l].astype(F32)),
            diff_scalars=jnp.concatenate([jnp.stack([lam, jnp.asarray(1.0 - lam_init, F32)]), slopes]).astype(F32),
            subln_g=subln_g[l][None, :],
            w_proj_a=w_proj_a[l].astype(BF16), w_proj_b=w_proj_b[l].astype(BF16),
            w_proj_c=w_proj_c[l].astype(BF16), w_out=w_out[l].astype(BF16),
            ln1_g=ln1_g[l][None, :], ln1_b=ln1_b[l][None, :],
            w_router=w_router[l], b_router=b_router[l][None, :],
            w1p=_w1_prep(w1, l, td=w1_prep_tiles[0], tc=w1_prep_tiles[1]),
            b1g=b1[l, :, None, 0::2], b1l=b1[l, :, None, 1::2],
            w2=w2_bf16, layer=l, b2=b2[l][:, None, :],
            ln2_g=ln2_g[l][None, :], ln2_b=ln2_b[l][None, :],
        ))
    return layers


def _encoder_layer(xs, p, groups, *, alpha, cfg):
    d = xs[0].shape[1]
    proj = _in_proj(xs, p["w_in"], cfg["proj_tm"], cfg["proj_tn"])
    na_w = cfg["na_heads"] * LANES
    rnn_w = p["lru_ba"].shape[1]
    dq = cfg["diff_heads"] * 2 * cfg["diff_dc"]
    c_qa = 3 * d
    c_xb = c_qa + 3 * na_w
    c_qc = c_xb + 2 * rnn_w
    oa, ob, oc = [], [], []
    for row0, batch, seq in groups:
        oa.append(_na_attention(proj, p["na_bias"][_na_table_rows(seq)], row0=row0, batch=batch, seq=seq, heads=cfg["na_heads"],
                                col_q=c_qa, col_k=c_qa + na_w, col_v=c_qa + 2 * na_w))
        ob.append(_rglru(proj, p["conv_w"], p["conv_b"], p["lru_wa"], p["lru_ba"], p["lru_wx"], p["lru_bx"],
                         p["lru_log_scale"], row0=row0, batch=batch, seq=seq, col_x=c_xb, col_y=c_xb + rnn_w))
        oc.append(_diff_attention(proj, p["diff_scalars"], p["subln_g"], row0=row0, batch=batch, seq=seq,
                                  heads=cfg["diff_heads"], dc=cfg["diff_dc"], col_q=c_qc, col_k=c_qc + dq,
                                  col_v=c_qc + 2 * dq, tq=cfg["diff_tq"]))
    x1, x1p = _merge_out_ln(xs, proj, oa, ob, oc, p["w_proj_a"], p["w_proj_b"], p["w_proj_c"], p["w_out"],
                            p["ln1_g"], p["ln1_b"], alpha=alpha, tm=cfg["merge_tm"])
    yp, dest, top_w = _moe(x1, x1p, p, tm=cfg["moe_tm"], tf=cfg["moe_tf"], tt_router=cfg["router_tt"])
    return tuple(_combine_ln(x1, yp, dest, top_w, p["ln2_g"], p["ln2_b"], alpha=alpha, tt=cfg["combine_tt"],
                             row0=row0, nrows=batch * seq) for row0, batch, seq in groups)


DEFAULT_CFG = dict(proj_tm=1024, proj_tn=1024, na_heads=4, diff_heads=4, diff_dc=64, diff_tq=256, merge_tm=256,
                   moe_tm=512, moe_tf=1024, router_tt=512, combine_tt=256, w1_prep_tiles=(512, 2048))


def _trunk(xs, params, cfg):
    d = xs[0].shape[-1]
    groups, row0 = [], 0
    for x in xs:
        groups.append((row0, x.shape[0], x.shape[1]))
        row0 += x.shape[0] * x.shape[1]
    na_rows = sorted({_na_table_rows(s) for _, _, s in groups})
    layers = _prepare_params(*params, d_model=d, na_rows=na_rows, diff_heads=cfg["diff_heads"],
                             w1_prep_tiles=cfg["w1_prep_tiles"])
    alpha = (2 * len(layers)) ** 0.25
    streams = tuple(x.reshape(-1, d) for x in xs)
    for p in layers:
        streams = _encoder_layer(streams, p, tuple(groups), alpha=alpha, cfg=cfg)
    return tuple(o.reshape(xin.shape) for o, xin in zip(streams, xs))


def kernel(x_prompt, x_sample, w_in, rpb_a, conv_w, conv_b, lru_wa, lru_ba, lru_wx, lru_bx, lru_lambda, lam_q1,
           lam_k1, lam_q2, lam_k2, subln_g, w_proj_a, w_proj_b, w_proj_c, w_out, ln1_g, ln1_b, w_router,
           b_router, w1, b1, w2, b2, ln2_g, ln2_b):
    params = (w_in, rpb_a, conv_w, conv_b, lru_wa, lru_ba, lru_wx, lru_bx, lru_lambda, lam_q1, lam_k1, lam_q2,
              lam_k2, subln_g, w_proj_a, w_proj_b, w_proj_c, w_out, ln1_g, ln1_b, w_router, b_router, w1, b1,
              w2, b2, ln2_g, ln2_b)
    return _trunk((x_prompt, x_sample), params, DEFAULT_CFG)
```

```python
import functools
import math

import jax
import jax.numpy as jnp
from jax import lax
from jax.experimental import pallas as pl
from jax.experimental.pallas import tpu as pltpu

F32 = jnp.float32
BF16 = jnp.bfloat16
I32 = jnp.int32

GRID_W = 64
NA_KH_MAX = 8
NA_KW = 16
NA_GROUP = 8
EXPERT_LOOKAHEAD = 2
COMBINE_LOOKAHEAD = 2
LRU_C = 8.0
TOP_K = 4
SWIGLU_LIMIT = 7.0
SWIGLU_ALPHA = 1.702
LN_EPS = 1e-5
NEG_INF = -1e30
LOG2E = math.log2(math.e)

LANES = 128
SUBLANES = 8
VMEM_LIMIT = 56 * 1024 * 1024


def _cparams(*sem):
    return pltpu.CompilerParams(dimension_semantics=sem, vmem_limit_bytes=VMEM_LIMIT)


def _resident(block_shape, index_map):
    return pl.BlockSpec(block_shape, index_map, pipeline_mode=pl.Buffered(1))


def _stream_specs(xs, tm):
    offsets, specs, off = [], [], 0
    for x in xs:
        assert x.shape[0] % tm == 0
        nb = x.shape[0] // tm
        offsets.append(off)
        specs.append(pl.BlockSpec((tm, x.shape[1]),
                                  lambda *a, off=off, nb=nb: (jnp.clip(a[0] - off, 0, nb - 1), 0)))
        off += nb
    return tuple(offsets), specs, off


def _stream_tile(x_refs, offsets, i):
    x = x_refs[0][...]
    for ref, off in zip(x_refs[1:], offsets[1:]):
        x = jnp.where(i >= off, ref[...], x)
    return x


def _in_proj_kernel(*refs, offsets):
    x_refs, (w_ref, o_ref, xb_ref) = refs[:len(offsets)], refs[len(offsets):]

    @pl.when(pl.program_id(1) == 0)
    def _():
        xb_ref[...] = _stream_tile(x_refs, offsets, pl.program_id(0)).astype(BF16)

    o_ref[...] = jnp.dot(xb_ref[...], w_ref[...], preferred_element_type=F32).astype(o_ref.dtype)


def _in_proj(xs, w, tm, tn):
    d = xs[0].shape[1]
    width = w.shape[1]
    offsets, x_specs, n_blocks = _stream_specs(xs, tm)
    return pl.pallas_call(
        functools.partial(_in_proj_kernel, offsets=offsets),
        grid=(n_blocks, width // tn),
        in_specs=x_specs + [pl.BlockSpec((d, tn), lambda i, j: (0, j))],
        out_specs=pl.BlockSpec((tm, tn), lambda i, j: (i, j)),
        out_shape=jax.ShapeDtypeStruct((n_blocks * tm, width), BF16),
        scratch_shapes=[pltpu.VMEM((tm, d), BF16)],
        compiler_params=_cparams("parallel", "arbitrary"),
        name="in_proj",
    )(*xs, w)


def _na_bias_table(rpb, rows):
    kh, grp = NA_KH_MAX, NA_GROUP
    span = min(2 * grp, rows)
    c = jnp.arange(GRID_W)[:, None]
    kc = jnp.arange(GRID_W)[None, :]
    col_off = jnp.clip(kc - c + NA_KW - 1, 0, 2 * NA_KW - 2)
    win = jnp.clip(c - NA_KW // 2, 0, GRID_W - NA_KW)
    ok = (kc >= win) & (kc < win + NA_KW)
    onehot = (col_off[..., None] == jnp.arange(2 * NA_KW - 1)).astype(F32)
    cols = jnp.einsum("hrv,ckv->hrck", rpb.astype(F32), onehot, precision=lax.Precision.HIGHEST)
    cols = jnp.where(ok, cols, NEG_INF)
    v = jnp.arange(3)[:, None, None]
    g = jnp.arange(grp)[None, :, None]
    ru = jnp.arange(span)[None, None, :]
    u0 = jnp.where(v == 0, 0, jnp.where(v == 1, kh // 2, rows - span))
    first = jnp.clip(u0 + (kh // 2) * v + g - kh // 2, 0, rows - kh) - u0
    valid = (ru >= first) & (ru < first + kh)
    row_off = jnp.clip(ru - (kh // 2) * v - g + kh - 1, 0, 2 * kh - 2)
    tab = jnp.where(valid[None, :, :, :, None, None], cols[:, row_off], NEG_INF)
    tab = tab * LOG2E
    return tab.transpose(0, 1, 2, 4, 3, 5).reshape(rpb.shape[0], 3, grp * GRID_W, span * GRID_W)


def _na_table_rows(seq):
    return min(seq // GRID_W, 3 * NA_GROUP)


def _na_kernel(q_ref, k_ref, v_ref, bias_ref, o_ref, *, rows, scale):
    kh, grp = NA_KH_MAX, NA_GROUP
    span = min(2 * grp, rows)

    def group_body(gi, carry):
        i0 = gi * grp
        u0 = jnp.clip(i0 - kh // 2, 0, rows - span)
        qoff = pl.multiple_of(i0 * GRID_W, grp * GRID_W)
        koff = pl.multiple_of(u0 * GRID_W, (kh // 2) * GRID_W)
        q = q_ref[pl.ds(qoff, grp * GRID_W), :]
        kw = k_ref[pl.ds(koff, span * GRID_W), :]
        vw = v_ref[pl.ds(koff, span * GRID_W), :]
        s = lax.dot_general(q, kw, (((1,), (1,)), ((), ())), preferred_element_type=F32)
        s = s * (scale * LOG2E) + bias_ref[(i0 - u0) // (kh // 2)]
        m = jnp.max(s, axis=-1, keepdims=True)
        p = jnp.exp2(s - m)
        pv = jnp.dot(p.astype(BF16), _with_ones_column(vw), preferred_element_type=F32)
        o = pv[:, :LANES] / pv[:, LANES:LANES + 1]
        o_ref[pl.ds(qoff, grp * GRID_W), :] = o.astype(o_ref.dtype)
        return carry

    lax.fori_loop(0, rows // grp, group_body, 0, unroll=2)


def _na_attention(proj, bias, *, row0, batch, seq, heads, col_q, col_k, col_v):
    hd = LANES
    rows = seq // GRID_W
    rb = row0 // seq
    assert row0 % seq == 0 and rows % NA_GROUP == 0 and rows >= NA_KH_MAX
    spec = lambda col: pl.BlockSpec((seq, hd), lambda b, h: (rb + b, col // hd + h))
    return pl.pallas_call(
        functools.partial(_na_kernel, rows=rows, scale=hd ** -0.5),
        grid=(batch, heads),
        in_specs=[spec(col_q), spec(col_k), spec(col_v),
                  pl.BlockSpec((None,) + bias.shape[1:], lambda b, h: (h, 0, 0, 0))],
        out_specs=pl.BlockSpec((seq, hd), lambda b, h: (b, h)),
        out_shape=jax.ShapeDtypeStruct((batch * seq, heads * hd), BF16),
        compiler_params=_cparams("parallel", "parallel"),
        name="na_attention",
    )(proj, proj, proj, bias)


def _lru_kernel(x_ref, y_ref, cw_ref, cb_ref, wa_ref, ba_ref, wx_ref, bx_ref, ls_ref, o_ref,
                a0_ref, b0_ref, a1_ref, b1_ref, *, seq):
    x = x_ref[...].astype(F32)
    t_idx = lax.broadcasted_iota(I32, x.shape, 0)
    xc = (cw_ref[0:1, :] * jnp.where(t_idx >= 2, pltpu.roll(x, 2, 0), 0.0)
          + cw_ref[1:2, :] * jnp.where(t_idx >= 1, pltpu.roll(x, 1, 0), 0.0)
          + cw_ref[2:3, :] * x
          + cw_ref[3:4, :] * jnp.where(t_idx < seq - 1, pltpu.roll(x, seq - 1, 0), 0.0)
          + cb_ref[...])
    xb = xc.astype(BF16)
    n_tiles = seq // SUBLANES
    row3 = lax.broadcasted_iota(I32, (n_tiles, SUBLANES, LANES), 1)

    def tile_prefix(a, b, reverse):
        a = a.reshape(n_tiles, SUBLANES, LANES)
        b = b.reshape(n_tiles, SUBLANES, LANES)
        for s in (1, 2, 4):
            if reverse:
                keep = row3 < SUBLANES - s
                shift = SUBLANES - s
            else:
                keep = row3 >= s
                shift = s
            a_sh = jnp.where(keep, pltpu.roll(a, shift, 1), 1.0)
            b_sh = jnp.where(keep, pltpu.roll(b, shift, 1), 0.0)
            b = a * b_sh + b
            a = a * a_sh
        return a.reshape(seq, LANES), b.reshape(seq, LANES)

    for d, (a_ref, b_ref) in enumerate(((a0_ref, b0_ref), (a1_ref, b1_ref))):
        r = jax.nn.sigmoid(jnp.dot(xb, wa_ref[d], preferred_element_type=F32) + ba_ref[d:d + 1, :])
        ig = jax.nn.sigmoid(jnp.dot(xb, wx_ref[d], preferred_element_type=F32) + bx_ref[d:d + 1, :])
        a = jnp.exp(ls_ref[d:d + 1, :] * r)
        b = jnp.sqrt(1.0 - a * a) * (ig * xc)
        a_ref[...], b_ref[...] = tile_prefix(a, b, reverse=(d == 1))

    def tile_apply(a, b, carry, reverse):
        e = 0 if reverse else SUBLANES - 1
        a_edge = jnp.broadcast_to(a[e:e + 1, :], (SUBLANES, LANES))
        b_edge = jnp.broadcast_to(b[e:e + 1, :], (SUBLANES, LANES))
        return b + a * carry, b_edge + a_edge * carry

    def scan_body(v, carry):
        cf, cb = carry
        rf = pl.multiple_of(v * SUBLANES, SUBLANES)
        rb = pl.multiple_of((n_tiles - 1 - v) * SUBLANES, SUBLANES)
        hf, cf = tile_apply(a0_ref[pl.ds(rf, SUBLANES), :], b0_ref[pl.ds(rf, SUBLANES), :], cf, False)
        hb, cb = tile_apply(a1_ref[pl.ds(rb, SUBLANES), :], b1_ref[pl.ds(rb, SUBLANES), :], cb, True)
        b0_ref[pl.ds(rf, SUBLANES), :] = hf
        b1_ref[pl.ds(rb, SUBLANES), :] = hb
        return cf, cb

    zeros = jnp.zeros((SUBLANES, LANES), F32)
    lax.fori_loop(0, n_tiles, scan_body, (zeros, zeros), unroll=8)

    h = b0_ref[...] + b1_ref[...]
    y = y_ref[...].astype(F32)
    o_ref[...] = (h * jax.nn.gelu(y, approximate=True)).astype(o_ref.dtype)


def _rglru(proj, conv_w, conv_b, wa, ba, wx, bx, log_scale, *, row0, batch, seq, col_x, col_y):
    nb = wa.shape[1]
    bw = wa.shape[2]
    assert bw == LANES and row0 % seq == 0 and seq % (8 * SUBLANES) == 0
    rb = row0 // seq
    cvec = lambda rows_: pl.BlockSpec((rows_, bw), lambda b, n: (0, n))
    wspec = pl.BlockSpec((2, None, bw, bw), lambda b, n: (0, n, 0, 0))
    return pl.pallas_call(
        functools.partial(_lru_kernel, seq=seq),
        grid=(batch, nb),
        in_specs=[pl.BlockSpec((seq, bw), lambda b, n: (rb + b, col_x // bw + n)),
                  pl.BlockSpec((seq, bw), lambda b, n: (rb + b, col_y // bw + n)),
                  cvec(conv_w.shape[0]), cvec(1), wspec, cvec(2), wspec, cvec(2), cvec(2)],
        out_specs=pl.BlockSpec((seq, bw), lambda b, n: (b, n)),
        out_shape=jax.ShapeDtypeStruct((batch * seq, nb * bw), BF16),
        scratch_shapes=[pltpu.VMEM((seq, bw), F32)] * 4,
        compiler_params=_cparams("parallel", "parallel"),
        name="rglru",
    )(proj, proj, conv_w, conv_b, wa, ba, wx, bx, log_scale)


def _with_ones_column(v):
    lane = lax.broadcasted_iota(I32, v.shape, 1)
    ones_col = jnp.where(lane == 0, 1.0, 0.0).astype(v.dtype)
    return jnp.concatenate([v, ones_col], axis=1)


def _diff_kernel(sc_ref, q_ref, k_ref, v_ref, g_ref, o_ref, *, tq, seq, dc, scale):
    h = pl.program_id(1)
    lam = sc_ref[0]
    out_scale = sc_ref[1]
    slope = sc_ref[2 + h]
    q = q_ref[...]
    k = k_ref[...]
    v = v_ref[...]
    lane = lax.broadcasted_iota(I32, q.shape, 1)
    qpos = pl.program_id(2) * tq + lax.broadcasted_iota(I32, (tq, 1), 0)
    kpos = lax.broadcasted_iota(I32, (1, seq), 1)
    bias = jnp.abs(qpos - kpos).astype(F32) * (-slope * LOG2E)
    v_ones = _with_ones_column(v)
    outs = []
    for m_idx in range(2):
        in_map = (lane < dc) if m_idx == 0 else (lane >= dc)
        qm = jnp.where(in_map, q, jnp.zeros_like(q))
        qm = (qm.astype(F32) * (scale * LOG2E)).astype(BF16)
        s = lax.dot_general(qm, k, (((1,), (1,)), ((), ())), preferred_element_type=F32) + bias
        mx = jnp.max(s, axis=-1, keepdims=True)
        p = jnp.exp2(s - mx)
        pv = jnp.dot(p.astype(BF16), v_ones, preferred_element_type=F32)
        outs.append(pv[:, :LANES] / pv[:, LANES:LANES + 1])
    o = outs[0] - lam * outs[1]
    o = o * lax.rsqrt(jnp.mean(o * o, axis=-1, keepdims=True) + LN_EPS)
    o_ref[...] = (o * g_ref[...] * out_scale).astype(o_ref.dtype)


def _diff_attention(proj, scalars, subln_g, *, row0, batch, seq, heads, dc, col_q, col_k, col_v, tq):
    hd = 2 * dc
    assert hd == LANES and row0 % seq == 0 and seq % tq == 0
    nq = seq // tq
    rb = row0 // seq
    return pl.pallas_call(
        functools.partial(_diff_kernel, tq=tq, seq=seq, dc=dc, scale=dc ** -0.5),
        grid=(batch, heads, nq),
        in_specs=[pl.BlockSpec(memory_space=pltpu.SMEM),
                  pl.BlockSpec((tq, hd), lambda b, h, i: ((rb + b) * nq + i, col_q // hd + h)),
                  pl.BlockSpec((seq, hd), lambda b, h, i: (rb + b, col_k // hd + h)),
                  pl.BlockSpec((seq, hd), lambda b, h, i: (rb + b, col_v // hd + h)),
                  pl.BlockSpec((1, hd), lambda b, h, i: (0, 0))],
        out_specs=pl.BlockSpec((tq, hd), lambda b, h, i: (b * nq + i, h)),
        out_shape=jax.ShapeDtypeStruct((batch * seq, heads * hd), BF16),
        compiler_params=_cparams("parallel", "parallel", "arbitrary"),
        name="diff_attention",
    )(scalars, proj, proj, proj, subln_g)


def _layer_norm(y, g, b):
    mu = jnp.mean(y, axis=-1, keepdims=True)
    yc = y - mu
    var = jnp.mean(yc * yc, axis=-1, keepdims=True)
    return yc * lax.rsqrt(var + LN_EPS) * g + b


def _packed_rows(d):
    return d // (2 * LANES)


def _store_packed(ref, y):
    m, d = y.shape
    pr = _packed_rows(d)
    lo = lax.bitcast_convert_type(y[:, :d // 2].astype(BF16).astype(F32), jnp.uint32)
    hi = lax.bitcast_convert_type(y[:, d // 2:].astype(BF16).astype(F32), jnp.uint32)
    word = hi | (lo >> 16)
    for c in range(pr):
        ref[pl.ds(c, m, stride=pr), :] = word[:, c * LANES:(c + 1) * LANES]


def _load_packed(ref, c, m, pr, row0=0):
    word = ref[pl.ds(row0 * pr + c, m, stride=pr), :]
    lo = lax.bitcast_convert_type(word << 16, F32)
    hi = lax.bitcast_convert_type(word & jnp.uint32(0xFFFF0000), F32)
    return lo, hi


def _merge_kernel(*refs, d, alpha, offsets):
    g = len(offsets)
    i = pl.program_id(0)
    x_refs, oa_refs, ob_refs, oc_refs = (refs[j * g:(j + 1) * g] for j in range(4))
    g_ref, wa_ref, wb_ref, wc_ref, wo_ref, lg_ref, lb_ref, o_ref, op_ref = refs[4 * g:]
    x = _stream_tile(x_refs, offsets, i)
    merged = None
    for idx, (o_refs, w_br) in enumerate(((oa_refs, wa_ref), (ob_refs, wb_ref), (oc_refs, wc_ref))):
        gate = jax.nn.sigmoid(g_ref[:, idx * d:(idx + 1) * d].astype(F32))
        term = gate * jnp.dot(_stream_tile(o_refs, offsets, i), w_br[...], preferred_element_type=F32)
        merged = term if merged is None else merged + term
    y = alpha * x + jnp.dot(merged.astype(BF16), wo_ref[...], preferred_element_type=F32)
    y = _layer_norm(y, lg_ref[...], lb_ref[...])
    o_ref[...] = y
    _store_packed(op_ref, y)


def _merge_out_ln(xs, proj, oas, obs, ocs, wa, wb, wc, wo, ln_g, ln_b, *, alpha, tm):
    d = xs[0].shape[1]
    pr = _packed_rows(d)
    offsets, x_specs, n_blocks = _stream_specs(xs, tm)
    branch_specs = []
    for os_ in (oas, obs, ocs):
        branch_specs += _stream_specs(os_, tm)[1]
    n = n_blocks * tm
    row = lambda width: pl.BlockSpec((tm, width), lambda i: (i, 0))
    full = lambda arr: _resident(arr.shape, lambda i: (0, 0))
    return pl.pallas_call(
        functools.partial(_merge_kernel, d=d, alpha=alpha, offsets=offsets),
        grid=(n_blocks,),
        in_specs=x_specs + branch_specs + [row(3 * d), full(wa), full(wb), full(wc), full(wo), full(ln_g),
                                           full(ln_b)],
        out_specs=[row(d), pl.BlockSpec((tm * pr, LANES), lambda i: (i, 0))],
        out_shape=[jax.ShapeDtypeStruct((n, d), F32), jax.ShapeDtypeStruct((n * pr, LANES), jnp.uint32)],
        compiler_params=_cparams("parallel"),
        name="merge_out_ln1",
    )(*xs, *oas, *obs, *ocs, proj, wa, wb, wc, wo, ln_g, ln_b)


def _router_kernel(x_ref, w_ref, b_ref, idx_ref, wgt_ref, rank_ref, cnt_ref, carry_ref, *, tt, n_exp):
    @pl.when(pl.program_id(0) == 0)
    def _():
        carry_ref[...] = jnp.zeros_like(carry_ref)

    x = x_ref[...]
    w = w_ref[...]
    x_hi = x.astype(BF16)
    x_lo = (x - x_hi.astype(F32)).astype(BF16)
    w_hi = w.astype(BF16)
    w_lo = (w - w_hi.astype(F32)).astype(BF16)
    mm = functools.partial(jnp.dot, preferred_element_type=F32)
    logits = mm(x_hi, w_hi) + (mm(x_hi, w_lo) + mm(x_lo, w_hi)) + b_ref[...]
    lane = lax.broadcasted_iota(I32, (tt, n_exp), 1).astype(F32)
    work = logits
    vals, idxs, hots = [], [], []
    for _ in range(TOP_K):
        m = jnp.max(work, axis=-1, keepdims=True)
        idx = jnp.min(jnp.where(work == m, lane, float(n_exp)), axis=-1, keepdims=True)
        hot = lane == idx
        vals.append(m)
        idxs.append(idx)
        hots.append(hot)
        work = jnp.where(hot, -jnp.inf, work)
    exps = [jnp.exp(v - vals[0]) for v in vals]
    denom = exps[0]
    for e in exps[1:]:
        denom = denom + e
    onehot = jnp.zeros((tt, n_exp), F32)
    for hot in hots:
        onehot = onehot + hot.astype(F32)
    r_i = lax.broadcasted_iota(I32, (tt, tt), 0)
    c_i = lax.broadcasted_iota(I32, (tt, tt), 1)
    lower = jnp.where(r_i > c_i, 1.0, 0.0).astype(BF16)
    before = jnp.dot(lower, onehot.astype(BF16), preferred_element_type=F32) + carry_ref[...]
    for k in range(TOP_K):
        idx_ref[:, k:k + 1] = idxs[k].astype(I32)
        wgt_ref[:, k:k + 1] = exps[k] / denom
        rank_ref[:, k:k + 1] = jnp.sum(jnp.where(hots[k], before, 0.0), axis=-1, keepdims=True).astype(I32)
    carry_ref[...] = carry_ref[...] + jnp.sum(onehot, axis=0, keepdims=True)
    cnt_ref[...] = carry_ref[...].astype(I32)


def _router(x, w_router, b_router, *, tt):
    n, d = x.shape
    n_exp = w_router.shape[1]
    out4 = lambda dt: jax.ShapeDtypeStruct((n, TOP_K), dt)
    spec4 = pl.BlockSpec((tt, TOP_K), lambda i: (i, 0))
    return pl.pallas_call(
        functools.partial(_router_kernel, tt=tt, n_exp=n_exp),
        grid=(n // tt,),
        in_specs=[pl.BlockSpec((tt, d), lambda i: (i, 0)),
                  _resident((d, n_exp), lambda i: (0, 0)),
                  _resident((1, n_exp), lambda i: (0, 0))],
        out_specs=[spec4, spec4, spec4, pl.BlockSpec((1, n_exp), lambda i: (0, 0))],
        out_shape=[out4(I32), out4(F32), out4(I32), jax.ShapeDtypeStruct((1, n_exp), I32)],
        scratch_shapes=[pltpu.VMEM((1, n_exp), F32)],
        compiler_params=_cparams("arbitrary"),
        name="moe_router",
    )(x, w_router, b_router)


def _w1_prep_kernel(w_ref, o_ref):
    grp = 2 * LANES
    r = lax.broadcasted_iota(I32, (grp, grp), 0)
    c = lax.broadcasted_iota(I32, (grp, grp), 1)
    src = jnp.where(c < LANES, 2 * c, 2 * (c - LANES) + 1)
    perm = jnp.where(r == src, 1.0, 0.0).astype(BF16)
    for g in range(w_ref.shape[1] // grp):
        blk = w_ref[:, g * grp:(g + 1) * grp].astype(BF16)
        o_ref[:, g * grp:(g + 1) * grp] = jnp.dot(blk, perm, preferred_element_type=F32).astype(BF16)


def _w1_prep(w1, layer, *, td, tc):
    _, n_exp, d, two_f = w1.shape
    return pl.pallas_call(
        _w1_prep_kernel,
        grid=(n_exp, d // td, two_f // tc),
        in_specs=[pl.BlockSpec((None, None, td, tc), lambda e, i, j: (layer, e, i, j))],
        out_specs=pl.BlockSpec((None, td, tc), lambda e, i, j: (e, i, j)),
        out_shape=jax.ShapeDtypeStruct(w1.shape[1:], BF16),
        compiler_params=_cparams("parallel", "parallel", "parallel"),
        name="w1_prep",
    )(w1)


def _expert_kernel(te_ref, tv_ref, rt0_ref, rt1_ref, rta_ref, xp_ref, w1_ref, b1g_ref, b1l_ref, w2_ref, b2_ref,
                   o_ref, xg_ref, sem_ref, xb_ref, acc_ref, *, tm, tf, d, nf):
    i = pl.program_id(0)
    f = pl.program_id(1)
    last = nf - 1
    pr = _packed_rows(d)
    n_slots = EXPERT_LOOKAHEAD + 1
    slot = i % n_slots
    live = tv_ref[i] > 0
    fetched = (i < EXPERT_LOOKAHEAD) | (tv_ref[jnp.maximum(i - EXPERT_LOOKAHEAD, 0)] > 0)

    def row_copy(tok_ref, r, slot_):
        src = xp_ref.at[pl.ds(pl.multiple_of(tok_ref[r] * pr, pr), pr)]
        dst = xg_ref.at[slot_, pl.ds(pl.multiple_of(r * pr, pr), pr)]
        return pltpu.make_async_copy(src, dst, sem_ref.at[slot_])

    @pl.when((i == 0) & (f == 0))
    def _():
        for tile, tok_ref in enumerate((rt0_ref, rt1_ref)):
            def body(r, carry, tile=tile, tok_ref=tok_ref):
                row_copy(tok_ref, r, tile).start()
                return carry
            lax.fori_loop(0, tm, body, 0, unroll=8)

    @pl.when(fetched & (f == 0))
    def _():
        pltpu.make_async_copy(xg_ref.at[slot], xg_ref.at[slot], sem_ref.at[slot]).wait()

    @pl.when(live & (f == 0))
    def _():
        for c in range(pr):
            lo, hi = _load_packed(xg_ref.at[slot], c, tm, pr)
            xb_ref[:, c * LANES:(c + 1) * LANES] = lo.astype(BF16)
            xb_ref[:, d // 2 + c * LANES:d // 2 + (c + 1) * LANES] = hi.astype(BF16)
        acc_ref[...] = jnp.zeros_like(acc_ref)

    @pl.when(live)
    def _():
        chunk = tm // nf
        ahead_slot = (i + EXPERT_LOOKAHEAD) % n_slots
        for r in range(chunk):
            row_copy(rta_ref, f * chunk + r, ahead_slot).start()
        h = jnp.dot(xb_ref[...], w1_ref[...], preferred_element_type=F32)
        acts = []
        for j in range(tf // LANES):
            hg = h[:, 2 * j * LANES:(2 * j + 1) * LANES] + b1g_ref[:, j * LANES:(j + 1) * LANES]
            hl = h[:, (2 * j + 1) * LANES:(2 * j + 2) * LANES] + b1l_ref[:, j * LANES:(j + 1) * LANES]
            glu = jnp.minimum(hg, SWIGLU_LIMIT)
            lin = jnp.clip(hl, -SWIGLU_LIMIT, SWIGLU_LIMIT)
            acts.append((glu * jax.nn.sigmoid(SWIGLU_ALPHA * glu) * (lin + 1.0)).astype(BF16))
        act = jnp.concatenate(acts, axis=1)
        acc_ref[...] += jnp.dot(act, w2_ref[...], preferred_element_type=F32)

    @pl.when(live & (f == last))
    def _():
        _store_packed(o_ref, acc_ref[...] + b2_ref[...])

    @pl.when(jnp.logical_not(live) & (f == last))
    def _():
        o_ref[...] = jnp.zeros_like(o_ref)


def _expert_mlp(xp, row_tok, tile_expert, tile_valid, w1p, b1g, b1l, w2, layer, b2, *, tm, tf):
    d = w2.shape[3]
    pr = _packed_rows(d)
    n_tiles = row_tok.shape[0] // tm
    nf = w2.shape[2] // tf
    fidx = lambda f, tv, i: jnp.where(tv[i] > 0, f, nf - 1)
    grid_spec = pltpu.PrefetchScalarGridSpec(
        num_scalar_prefetch=2,
        grid=(n_tiles, nf),
        in_specs=[pl.BlockSpec((tm,), lambda i, f, te, tv: (0,), memory_space=pltpu.SMEM),
                  pl.BlockSpec((tm,), lambda i, f, te, tv: (1,), memory_space=pltpu.SMEM),
                  pl.BlockSpec((tm,), lambda i, f, te, tv: (jnp.minimum(i + EXPERT_LOOKAHEAD, n_tiles - 1),),
                               memory_space=pltpu.SMEM),
                  pl.BlockSpec(memory_space=pl.ANY),
                  pl.BlockSpec((None, d, 2 * tf), lambda i, f, te, tv: (te[i], 0, fidx(f, tv, i))),
                  pl.BlockSpec((None, 1, tf), lambda i, f, te, tv: (te[i], 0, fidx(f, tv, i))),
                  pl.BlockSpec((None, 1, tf), lambda i, f, te, tv: (te[i], 0, fidx(f, tv, i))),
                  pl.BlockSpec((None, None, tf, d), lambda i, f, te, tv: (layer, te[i], fidx(f, tv, i), 0)),
                  pl.BlockSpec((None, 1, d), lambda i, f, te, tv: (te[i], 0, 0))],
        out_specs=pl.BlockSpec((tm * pr, LANES), lambda i, f, te, tv: (i, 0)),
        scratch_shapes=[pltpu.VMEM((EXPERT_LOOKAHEAD + 1, tm * pr, LANES), jnp.uint32),
                        pltpu.SemaphoreType.DMA((EXPERT_LOOKAHEAD + 1,)),
                        pltpu.VMEM((tm, d), BF16), pltpu.VMEM((tm, d), F32)],
    )
    return pl.pallas_call(
        functools.partial(_expert_kernel, tm=tm, tf=tf, d=d, nf=nf),
        grid_spec=grid_spec,
        out_shape=jax.ShapeDtypeStruct((n_tiles * tm * pr, LANES), jnp.uint32),
        compiler_params=_cparams("arbitrary", "arbitrary"),
        name="expert_mlp",
    )(tile_expert, tile_valid, row_tok, row_tok, row_tok, xp, w1p, b1g, b1l, w2, b2)


def _combine_kernel(d0_ref, d1_ref, da_ref, x_ref, yp_ref, w_ref, lg_ref, lb_ref, o_ref, yg_ref, sem_ref,
                    ysc_ref, *, alpha, tt, d):
    i = pl.program_id(0)
    n_steps = pl.num_programs(0)
    pr = _packed_rows(d)
    n_slots = COMBINE_LOOKAHEAD + 1
    slot = i % n_slots

    def start_rows(dest_ref, t, slot_):
        for k in range(TOP_K):
            src = yp_ref.at[pl.ds(pl.multiple_of(dest_ref[t * TOP_K + k] * pr, pr), pr)]
            dst = yg_ref.at[slot_, pl.ds(pl.multiple_of((k * tt + t) * pr, pr), pr)]
            pltpu.make_async_copy(src, dst, sem_ref.at[slot_]).start()

    def wait_slot(slot_):
        pltpu.make_async_copy(yg_ref.at[slot_], yg_ref.at[slot_], sem_ref.at[slot_]).wait()

    @pl.when(i == 0)
    def _():
        for tile, dest_ref in enumerate((d0_ref, d1_ref)):
            def body(t, carry, tile=tile, dest_ref=dest_ref):
                start_rows(dest_ref, t, tile)
                return carry
            lax.fori_loop(0, tt, body, 0, unroll=2)

    wait_slot(slot)

    ahead_slot = (i + COMBINE_LOOKAHEAD) % n_slots
    tokens_per_group = tt // pr
    for c in range(pr):
        for t in range(c * tokens_per_group, (c + 1) * tokens_per_group):
            start_rows(da_ref, t, ahead_slot)
        lo_cols = slice(c * LANES, (c + 1) * LANES)
        hi_cols = slice(d // 2 + c * LANES, d // 2 + (c + 1) * LANES)
        acc_lo = alpha * x_ref[:, lo_cols]
        acc_hi = alpha * x_ref[:, hi_cols]
        for k in range(TOP_K):
            lo, hi = _load_packed(yg_ref.at[slot], c, tt, pr, row0=k * tt)
            wk = w_ref[:, k:k + 1]
            acc_lo = acc_lo + wk * lo
            acc_hi = acc_hi + wk * hi
        ysc_ref[:, lo_cols] = acc_lo
        ysc_ref[:, hi_cols] = acc_hi
    o_ref[...] = _layer_norm(ysc_ref[...], lg_ref[...], lb_ref[...])

    @pl.when(i == n_steps - 1)
    def _():
        for ahead in range(1, COMBINE_LOOKAHEAD + 1):
            wait_slot((i + ahead) % n_slots)


def _combine_ln(x, yp, dest, top_w, ln_g, ln_b, *, alpha, tt, row0, nrows):
    d = x.shape[1]
    pr = _packed_rows(d)
    assert row0 % tt == 0 and nrows % tt == 0
    n_steps = nrows // tt
    b0 = row0 // tt
    return pl.pallas_call(
        functools.partial(_combine_kernel, alpha=alpha, tt=tt, d=d),
        grid=(n_steps,),
        in_specs=[pl.BlockSpec((tt * TOP_K,), lambda i: (b0,), memory_space=pltpu.SMEM),
                  pl.BlockSpec((tt * TOP_K,), lambda i: (b0 + min(1, n_steps - 1),), memory_space=pltpu.SMEM),
                  pl.BlockSpec((tt * TOP_K,), lambda i: (b0 + jnp.minimum(i + COMBINE_LOOKAHEAD, n_steps - 1),),
                               memory_space=pltpu.SMEM),
                  pl.BlockSpec((tt, d), lambda i: (b0 + i, 0)),
                  pl.BlockSpec(memory_space=pl.ANY),
                  pl.BlockSpec((tt, TOP_K), lambda i: (b0 + i, 0)),
                  _resident((1, d), lambda i: (0, 0)),
                  _resident((1, d), lambda i: (0, 0))],
        out_specs=pl.BlockSpec((tt, d), lambda i: (i, 0)),
        out_shape=jax.ShapeDtypeStruct((nrows, d), F32),
        scratch_shapes=[pltpu.VMEM((COMBINE_LOOKAHEAD + 1, TOP_K * tt * pr, LANES), jnp.uint32),
                        pltpu.SemaphoreType.DMA((COMBINE_LOOKAHEAD + 1,)), pltpu.VMEM((tt, d), F32)],
        compiler_params=_cparams("arbitrary"),
        name="combine_ln2",
    )(dest, dest, dest, x, yp, top_w, ln_g, ln_b)


def _moe_layout(top_i, rank, counts, *, tm):
    n, k = top_i.shape
    n_exp = counts.shape[0]
    n_tiles = (n * k) // tm + n_exp + EXPERT_LOOKAHEAD
    padded = (counts + tm - 1) // tm * tm
    pends = jnp.cumsum(padded)
    pstarts = pends - padded
    first_row = jnp.sum(jnp.where(top_i[..., None] == jnp.arange(n_exp, dtype=I32), pstarts.astype(I32), 0), -1)
    dest = first_row + rank
    tile_start = jnp.arange(n_tiles, dtype=I32) * tm
    tile_expert = jnp.sum(tile_start[:, None] >= pends[None, :], axis=1)
    tile_expert = jnp.clip(tile_expert, 0, n_exp - 1).astype(I32)
    tile_valid = jnp.clip(counts[tile_expert] - (tile_start - pstarts[tile_expert]), 0, tm).astype(I32)
    tok = jnp.broadcast_to(jnp.arange(n, dtype=I32)[:, None], (n, k))
    row_tok = jnp.zeros((n_tiles * tm,), I32).at[dest.reshape(-1)].set(
        tok.reshape(-1), unique_indices=True, mode="promise_in_bounds")
    return dest, row_tok, tile_expert, tile_valid


def _moe(x, xp, p, *, tm, tf, tt_router):
    top_i, top_w, rank, counts = _router(x, p["w_router"], p["b_router"], tt=tt_router)
    dest, row_tok, tile_expert, tile_valid = _moe_layout(top_i, rank, counts[0], tm=tm)
    yp = _expert_mlp(xp, row_tok, tile_expert, tile_valid, p["w1p"], p["b1g"], p["b1l"], p["w2"], p["layer"],
                     p["b2"], tm=tm, tf=tf)
    return yp, dest.reshape(-1), top_w


def _prepare_params(w_in, rpb_a, conv_w, conv_b, lru_wa, lru_ba, lru_wx, lru_bx, lru_lambda, lam_q1, lam_k1,
                    lam_q2, lam_k2, subln_g, w_proj_a, w_proj_b, w_proj_c, w_out, ln1_g, ln1_b, w_router,
                    b_router, w1, b1, w2, b2, ln2_g, ln2_b, *, d_model, na_rows, diff_heads, w1_prep_tiles):
    depth = w_in.shape[0]
    gate0 = w_in.shape[2] - 3 * d_model
    w2_bf16 = w2.astype(BF16)
    layers = []
    for l in range(depth):
        lam_init = 0.8 - 0.6 * math.exp(-0.3 * l)
        lam = (jnp.exp(jnp.sum(lam_q1[l].astype(F32) * lam_k1[l].astype(F32)))
               - jnp.exp(jnp.sum(lam_q2[l].astype(F32) * lam_k2[l].astype(F32))) + lam_init)
        slopes = 2.0 ** (-8.0 * (jnp.arange(diff_heads, dtype=F32) + 1.0) / diff_heads)
        layers.append(dict(
            w_in=jnp.concatenate([w_in[l, :, gate0:], w_in[l, :, :gate0]], axis=1).astype(BF16),
            na_bias={r: _na_bias_table(rpb_a[l], r) for r in na_rows},
            conv_w=conv_w[l], conv_b=conv_b[l][None, :],
            lru_wa=lru_wa[l].astype(BF16), lru_ba=lru_ba[l], lru_wx=lru_wx[l].astype(BF16), lru_bx=lru_bx[l],
            lru_log_scale=-LRU_C * jax.nn.softplus(-lru_lambda[l].astype(F32)),
            diff_scalars=jnp.concatenate([jnp.stack([lam, jnp.asarray(1.0 - lam_init, F32)]), slopes]).astype(F32),
            subln_g=subln_g[l][None, :],
            w_proj_a=w_proj_a[l].astype(BF16), w_proj_b=w_proj_b[l].astype(BF16),
            w_proj_c=w_proj_c[l].astype(BF16), w_out=w_out[l].astype(BF16),
            ln1_g=ln1_g[l][None, :], ln1_b=ln1_b[l][None, :],
            w_router=w_router[l], b_router=b_router[l][None, :],
            w1p=_w1_prep(w1, l, td=w1_prep_tiles[0], tc=w1_prep_tiles[1]),
            b1g=b1[l, :, None, 0::2], b1l=b1[l, :, None, 1::2],
            w2=w2_bf16, layer=l, b2=b2[l][:, None, :],
            ln2_g=ln2_g[l][None, :], ln2_b=ln2_b[l][None, :],
        ))
    return layers


def _encoder_layer(xs, p, groups, *, alpha, cfg):
    d = xs[0].shape[1]
    proj = _in_proj(xs, p["w_in"], cfg["proj_tm"], cfg["proj_tn"])
    na_w = cfg["na_heads"] * LANES
    rnn_w = p["lru_ba"].shape[1]
    dq = cfg["diff_heads"] * 2 * cfg["diff_dc"]
    c_qa = 3 * d
    c_xb = c_qa + 3 * na_w
    c_qc = c_xb + 2 * rnn_w
    oa, ob, oc = [], [], []
    for row0, batch, seq in groups:
        oa.append(_na_attention(proj, p["na_bias"][_na_table_rows(seq)], row0=row0, batch=batch, seq=seq, heads=cfg["na_heads"],
                                col_q=c_qa, col_k=c_qa + na_w, col_v=c_qa + 2 * na_w))
        ob.append(_rglru(proj, p["conv_w"], p["conv_b"], p["lru_wa"], p["lru_ba"], p["lru_wx"], p["lru_bx"],
                         p["lru_log_scale"], row0=row0, batch=batch, seq=seq, col_x=c_xb, col_y=c_xb + rnn_w))
        oc.append(_diff_attention(proj, p["diff_scalars"], p["subln_g"], row0=row0, batch=batch, seq=seq,
                                  heads=cfg["diff_heads"], dc=cfg["diff_dc"], col_q=c_qc, col_k=c_qc + dq,
                                  col_v=c_qc + 2 * dq, tq=cfg["diff_tq"]))
    x1, x1p = _merge_out_ln(xs, proj, oa, ob, oc, p["w_proj_a"], p["w_proj_b"], p["w_proj_c"], p["w_out"],
                            p["ln1_g"], p["ln1_b"], alpha=alpha, tm=cfg["merge_tm"])
    yp, dest, top_w = _moe(x1, x1p, p, tm=cfg["moe_tm"], tf=cfg["moe_tf"], tt_router=cfg["router_tt"])
    return tuple(_combine_ln(x1, yp, dest, top_w, p["ln2_g"], p["ln2_b"], alpha=alpha, tt=cfg["combine_tt"],
                             row0=row0, nrows=batch * seq) for row0, batch, seq in groups)


DEFAULT_CFG = dict(proj_tm=1024, proj_tn=1024, na_heads=4, diff_heads=4, diff_dc=64, diff_tq=256, merge_tm=256,
                   moe_tm=512, moe_tf=1024, router_tt=512, combine_tt=256, w1_prep_tiles=(512, 2048))


def _trunk(xs, params, cfg):
    d = xs[0].shape[-1]
    groups, row0 = [], 0
    for x in xs:
        groups.append((row0, x.shape[0], x.shape[1]))
        row0 += x.shape[0] * x.shape[1]
    na_rows = sorted({_na_table_rows(s) for _, _, s in groups})
    layers = _prepare_params(*params, d_model=d, na_rows=na_rows, diff_heads=cfg["diff_heads"],
                             w1_prep_tiles=cfg["w1_prep_tiles"])
    alpha = (2 * len(layers)) ** 0.25
    streams = tuple(x.reshape(-1, d) for x in xs)
    for p in layers:
        streams = _encoder_layer(streams, p, tuple(groups), alpha=alpha, cfg=cfg)
    return tuple(o.reshape(xin.shape) for o, xin in zip(streams, xs))


def kernel(x_prompt, x_sample, w_in, rpb_a, conv_w, conv_b, lru_wa, lru_ba, lru_wx, lru_bx, lru_lambda, lam_q1,
           lam_k1, lam_q2, lam_k2, subln_g, w_proj_a, w_proj_b, w_proj_c, w_out, ln1_g, ln1_b, w_router,
           b_router, w1, b1, w2, b2, ln2_g, ln2_b):
    params = (w_in, rpb_a, conv_w, conv_b, lru_wa, lru_ba, lru_wx, lru_bx, lru_lambda, lam_q1, lam_k1, lam_q2,
              lam_k2, subln_g, w_proj_a, w_proj_b, w_proj_c, w_out, ln1_g, ln1_b, w_router, b_router, w1, b1,
              w2, b2, ln2_g, ln2_b)
    return _trunk((x_prompt, x_sample), params, DEFAULT_CFG)
```
